```python
import jax, jax.numpy as jnp
from jax import lax
import numpy as np

D_MODEL = 4096
BATCH = 2
SEQ = 8192
DEPTH = 1

HEAD_DIM = 64
N_ATTN_HEADS = 32
N_RWKV_HEADS = 32
ATTN_WIDTH = N_ATTN_HEADS * HEAD_DIM
RWKV_WIDTH = N_RWKV_HEADS * HEAD_DIM
MIX_WIDTH = ATTN_WIDTH + RWKV_WIDTH
DILATION_PAIRS = ((128, 1), (512, 4), (2048, 16))
ATTN_BLOCK = 128
ROPE_THETA = 10000.0
DECAY_LORA = 128
ICLR_LORA = 128
GATE_LORA = 480
D_FF = 11008
CONV_WIDTH = 3
PLE_DIM = 256
NORM_EPS = 1e-6
RWKV_GN_EPS = 64e-5
RWKV_COLS = 3 * RWKV_WIDTH + DECAY_LORA + ICLR_LORA + GATE_LORA
IN_COLS = 3 * ATTN_WIDTH + RWKV_COLS

kernel_name = "hybrid_dilated_attn_rwkv7_convffn_ple"


def rmsnorm(x, g, eps=NORM_EPS):
    xf = x.astype(jnp.float32)
    y = xf * lax.rsqrt(jnp.mean(xf * xf, axis=-1, keepdims=True) + eps)
    return (y * g.astype(jnp.float32)).astype(x.dtype)


def apply_rope(t, positions):
    half = HEAD_DIM // 2
    inv_freq = ROPE_THETA ** (-jnp.arange(half, dtype=jnp.float32) / half)
    ang = positions.astype(jnp.float32)[:, :, None] * inv_freq
    cos = jnp.cos(ang)[:, :, None, :]
    sin = jnp.sin(ang)[:, :, None, :]
    tf = t.astype(jnp.float32)
    t1, t2 = tf[..., :half], tf[..., half:]
    return jnp.concatenate([t1 * cos - t2 * sin, t2 * cos + t1 * sin], axis=-1).astype(t.dtype)


def dilated_branch(q, k, v, window, dil):
    B, H, S, hd = q.shape
    n_back = window // dil
    span = dil * ATTN_BLOCK
    s_pad = -(-S // span) * span
    M = s_pad // dil
    nb = M // ATTN_BLOCK

    def to_strided(t):
        t = jnp.pad(t, ((0, 0), (0, 0), (0, s_pad - S), (0, 0)))
        t = t.reshape(B, H, M, dil, hd).transpose(0, 1, 3, 2, 4)
        return t.reshape(B, H, dil, nb, ATTN_BLOCK, hd)

    def with_prev_block(t):
        prev = jnp.pad(t[:, :, :, :-1], ((0, 0), (0, 0), (0, 0), (1, 0), (0, 0), (0, 0)))
        return jnp.concatenate([prev, t], axis=4)

    qs = to_strided(q)
    kc = with_prev_block(to_strided(k))
    vc = with_prev_block(to_strided(v))
    scores = jnp.einsum('bhrnqd,bhrnkd->bhrnqk', qs, kc).astype(jnp.float32) * (HEAD_DIM ** -0.5)

    qi = jnp.arange(ATTN_BLOCK)[:, None]
    ki = jnp.arange(2 * ATTN_BLOCK)[None, :]
    dist = ATTN_BLOCK + qi - ki
    band = (dist >= 0) & (dist <= n_back)
    has_prev = (jnp.arange(nb) > 0)[:, None, None] | (ki >= ATTN_BLOCK)[None]
    mask = band[None] & has_prev
    scores = jnp.where(mask[None, None, None], scores, -jnp.inf)

    lse = jax.nn.logsumexp(scores, axis=-1)
    probs = jnp.exp(scores - lse[..., None])
    o = jnp.einsum('bhrnqk,bhrnkd->bhrnqd', probs.astype(vc.dtype), vc)
    o = o.reshape(B, H, dil, M, hd).transpose(0, 1, 3, 2, 4).reshape(B, H, s_pad, hd)[:, :, :S]
    lse = lse.reshape(B, H, dil, M).transpose(0, 1, 3, 2).reshape(B, H, s_pad)[:, :, :S]
    return o, lse


def dilated_attention(q, k, v):
    outs, lses = [], []
    for window, dil in DILATION_PAIRS:
        o, lse = dilated_branch(q, k, v, window, dil)
        outs.append(o.astype(jnp.float32))
        lses.append(lse)
    wts = jax.nn.softmax(jnp.stack(lses, axis=0), axis=0)
    return jnp.sum(wts[..., None] * jnp.stack(outs, axis=0), axis=0)


def rwkv7_step(state, inp):
    r, w, k, v, a, b = inp
    sa = jnp.einsum('bhvk,bhk->bhv', state, a)
    state = state * w[:, :, None, :] + sa[..., None] * b[:, :, None, :] + v[..., None] * k[:, :, None, :]
    y = jnp.einsum('bhvk,bhk->bhv', state, r)
    return state, y


def rwkv7_time_mix(P, mu, w0, w_decay_up, a0, w_iclr_up, w_gate_up, k_k, k_a, r_k, ln_x_w, ln_x_b):
    B, S, _ = P.shape
    P = P.astype(jnp.float32)
    prev = jnp.pad(P[:, :-1], ((0, 0), (1, 0), (0, 0)))
    mixed = P + (prev - P) * mu
    cuts = [RWKV_WIDTH, 2 * RWKV_WIDTH, 3 * RWKV_WIDTH,
            3 * RWKV_WIDTH + DECAY_LORA, 3 * RWKV_WIDTH + DECAY_LORA + ICLR_LORA]
    r, k, v, wd, ad, gd = jnp.split(mixed, cuts, axis=-1)
    w_raw = -jax.nn.softplus(-(w0 + jnp.tanh(wd) @ w_decay_up)) - 0.5
    decay = jnp.exp(-jnp.exp(w_raw))
    a = jax.nn.sigmoid(a0 + ad @ w_iclr_up)
    g = jax.nn.sigmoid(gd) @ w_gate_up

    def heads(t):
        return t.reshape(B, S, N_RWKV_HEADS, HEAD_DIM)

    kk = heads(k * k_k)
    kk = kk / jnp.maximum(jnp.sqrt(jnp.sum(kk * kk, axis=-1, keepdims=True)), 1e-12)
    k = k * (1.0 + (a - 1.0) * k_a)
    r_h, k_h, v_h, w_h, a_h = heads(r), heads(k), heads(v), heads(decay), heads(a)
    xs = tuple(t.transpose(1, 0, 2, 3) for t in (r_h, w_h, k_h, v_h, -kk, kk * a_h))
    state0 = jnp.zeros((B, N_RWKV_HEADS, HEAD_DIM, HEAD_DIM), jnp.float32)
    _, y = lax.scan(rwkv7_step, state0, xs)
    y = y.transpose(1, 0, 2, 3)
    mean = jnp.mean(y, axis=-1, keepdims=True)
    var = jnp.mean(jnp.square(y - mean), axis=-1, keepdims=True)
    y = ((y - mean) * lax.rsqrt(var + RWKV_GN_EPS)).reshape(B, S, RWKV_WIDTH) * ln_x_w + ln_x_b
    bonus = (jnp.sum(r_h * k_h * r_k, axis=-1, keepdims=True) * v_h).reshape(B, S, RWKV_WIDTH)
    return (y + bonus) * g


def setup_inputs(seed: int = 0) -> dict:
    key = jax.random.key(seed)
    ks = jax.random.split(key, 32)
    L = DEPTH
    f32 = jnp.float32

    def nrm(k, shape, scale):
        return jax.random.normal(k, shape, f32) * scale

    def gain(k, shape):
        return 1.0 + 0.05 * jax.random.normal(k, shape, f32)

    x = jax.random.normal(ks[0], (BATCH, SEQ, D_MODEL), f32)
    p = jax.random.normal(ks[1], (L, BATCH, SEQ, PLE_DIM), f32)
    offset = jax.random.randint(ks[2], (BATCH, 1), 0, 4096, dtype=jnp.int32)
    positions = offset + jnp.arange(SEQ, dtype=jnp.int32)[None, :]
    return dict(
        x=x,
        p=p,
        positions=positions,
        attn_norm_g=gain(ks[3], (L, D_MODEL)),
        w_in=nrm(ks[4], (L, D_MODEL, IN_COLS), D_MODEL ** -0.5),
        q_norm_g=gain(ks[5], (L, HEAD_DIM)),
        k_norm_g=gain(ks[6], (L, HEAD_DIM)),
        rwkv_mu=jax.random.uniform(ks[7], (L, RWKV_COLS), f32),
        w0=jax.random.uniform(ks[8], (L, RWKV_WIDTH), f32, -3.0, 1.0),
        w_decay_up=nrm(ks[9], (L, DECAY_LORA, RWKV_WIDTH), 0.5 * DECAY_LORA ** -0.5),
        a0=nrm(ks[10], (L, RWKV_WIDTH), 0.1),
        w_iclr_up=nrm(ks[11], (L, ICLR_LORA, RWKV_WIDTH), ICLR_LORA ** -0.5),
        w_gate_up=nrm(ks[12], (L, GATE_LORA, RWKV_WIDTH), GATE_LORA ** -0.5),
        k_k=0.85 + 0.05 * jax.random.normal(ks[13], (L, RWKV_WIDTH), f32),
        k_a=gain(ks[14], (L, RWKV_WIDTH)),
        r_k=nrm(ks[15], (L, N_RWKV_HEADS, HEAD_DIM), 0.1),
        ln_x_w=gain(ks[16], (L, RWKV_WIDTH)),
        ln_x_b=nrm(ks[17], (L, RWKV_WIDTH), 0.01),
        w_out=nrm(ks[18], (L, MIX_WIDTH, D_MODEL), MIX_WIDTH ** -0.5),
        mlp_norm_g=gain(ks[19], (L, D_MODEL)),
        w_mlp_up=nrm(ks[20], (L, D_MODEL, 2 * D_FF), D_MODEL ** -0.5),
        conv_w=nrm(ks[21], (L, CONV_WIDTH, 2 * D_FF), CONV_WIDTH ** -0.5),
        conv_b=nrm(ks[22], (L, 2 * D_FF), 0.01),
        w_mlp_down=nrm(ks[23], (L, D_FF, D_MODEL), D_FF ** -0.5),
        w_ple_proj=nrm(ks[24], (L, PLE_DIM, D_MODEL), PLE_DIM ** -0.5),
        ple_norm_g=gain(ks[25], (L, D_MODEL)),
        w_ple_gate=nrm(ks[26], (L, D_MODEL, D_MODEL), D_MODEL ** -0.5),
    )


def reference(x, p, positions, attn_norm_g, w_in, q_norm_g, k_norm_g, rwkv_mu, w0, w_decay_up,
              a0, w_iclr_up, w_gate_up, k_k, k_a, r_k, ln_x_w, ln_x_b, w_out, mlp_norm_g,
              w_mlp_up, conv_w, conv_b, w_mlp_down, w_ple_proj, ple_norm_g, w_ple_gate):
    B, S, _ = x.shape
    h = x
    for i in range(DEPTH):
        xn = rmsnorm(h, attn_norm_g[i])
        proj = xn @ w_in[i]
        q, k, v = jnp.split(proj[..., :3 * ATTN_WIDTH], 3, axis=-1)
        q = q.reshape(B, S, N_ATTN_HEADS, HEAD_DIM)
        k = k.reshape(B, S, N_ATTN_HEADS, HEAD_DIM)
        v = v.reshape(B, S, N_ATTN_HEADS, HEAD_DIM)
        q = apply_rope(rmsnorm(q, q_norm_g[i]), positions)
        k = apply_rope(rmsnorm(k, k_norm_g[i]), positions)
        attn = dilated_attention(q.transpose(0, 2, 1, 3), k.transpose(0, 2, 1, 3),
                                 v.transpose(0, 2, 1, 3))
        attn = attn.transpose(0, 2, 1, 3).reshape(B, S, ATTN_WIDTH)
        rwkv = rwkv7_time_mix(proj[..., 3 * ATTN_WIDTH:], rwkv_mu[i], w0[i], w_decay_up[i], a0[i],
                              w_iclr_up[i], w_gate_up[i], k_k[i], k_a[i], r_k[i], ln_x_w[i], ln_x_b[i])
        mix = jnp.concatenate([attn.astype(x.dtype), rwkv.astype(x.dtype)], axis=-1)
        h = h + mix @ w_out[i]

        hn = rmsnorm(h, mlp_norm_g[i])
        u = hn @ w_mlp_up[i]
        u_pad = jnp.pad(u, ((0, 0), (CONV_WIDTH - 1, 0), (0, 0)))
        cw = conv_w[i].astype(u.dtype)
        u = sum(u_pad[:, j:j + S] * cw[j] for j in range(CONV_WIDTH)) + conv_b[i].astype(u.dtype)
        gate, up = jnp.split(u, 2, axis=-1)
        h = h + (jax.nn.silu(gate) * up) @ w_mlp_down[i]

        e = rmsnorm(p[i] @ w_ple_proj[i], ple_norm_g[i])
        h = h + jax.nn.sigmoid(h @ w_ple_gate[i]) * e
    return h
```

```python
import functools

import jax
import jax.numpy as jnp
from jax import lax
from jax.experimental import pallas as pl
from jax.experimental.pallas import tpu as pltpu

F32 = jnp.float32
BF16 = jnp.bfloat16

HEAD_DIM = 64
N_ATTN_HEADS = 32
N_RWKV_HEADS = 32
ATTN_WIDTH = N_ATTN_HEADS * HEAD_DIM
RWKV_WIDTH = N_RWKV_HEADS * HEAD_DIM
ATTN_BLOCK = 128
ATTN_WINDOW = 2048
ROPE_THETA = 10000.0
DECAY_LORA = 128
ICLR_LORA = 128
GATE_LORA = 480
GATE_LORA_PAD = 512
CONV_WIDTH = 3
NORM_EPS = 1e-6
RWKV_GN_EPS = 64e-5
LANES = 128
SUBLANES = 8
VMEM_LIMIT = 56 * 1024 * 1024
MASK_VALUE = -1e30


def _params(*semantics):
    return pltpu.CompilerParams(dimension_semantics=semantics, vmem_limit_bytes=VMEM_LIMIT)


def _rmsnorm_kernel(x_ref, g_ref, o_ref):
    x = x_ref[...]
    ms = jnp.mean(x * x, axis=-1, keepdims=True)
    o_ref[...] = (x * lax.rsqrt(ms + NORM_EPS) * g_ref[...]).astype(o_ref.dtype)


def _rmsnorm_bf16(x, g, *, tm=256):
    m, d = x.shape
    return pl.pallas_call(
        _rmsnorm_kernel,
        grid=(m // tm,),
        in_specs=[pl.BlockSpec((tm, d), lambda i: (i, 0)), pl.BlockSpec((1, d), lambda i: (0, 0))],
        out_specs=pl.BlockSpec((tm, d), lambda i: (i, 0)),
        out_shape=jax.ShapeDtypeStruct((m, d), BF16),
        compiler_params=_params("parallel"),
        name="rmsnorm_bf16",
    )(x, g.reshape(1, d))


def _matmul_kernel(*refs, has_res, has_gate, emit_bf16):
    x_ref, w_ref = refs[0], refs[1]
    pos = 2
    res_ref = e_ref = None
    if has_res:
        res_ref = refs[pos]
        pos += 1
    if has_gate:
        e_ref = refs[pos]
        pos += 1
    o_ref = refs[pos]
    acc = jnp.dot(x_ref[...], w_ref[...], preferred_element_type=F32)
    if has_gate:
        acc = jax.nn.sigmoid(acc) * e_ref[...]
    if has_res:
        acc = res_ref[...] + acc
    o_ref[...] = acc
    if emit_bf16:
        refs[pos + 1][...] = acc.astype(BF16)


def _matmul(x, w, *, tm, tn, residual=None, gate_e=None, emit_bf16=False, name):
    m, k = x.shape
    n = w.shape[1]
    tile = pl.BlockSpec((tm, tn), lambda i, j: (i, j))
    in_specs = [pl.BlockSpec((tm, k), lambda i, j: (i, 0)), pl.BlockSpec((k, tn), lambda i, j: (0, j))]
    args = [x, w]
    if residual is not None:
        in_specs.append(tile)
        args.append(residual)
    if gate_e is not None:
        in_specs.append(tile)
        args.append(gate_e)
    out_specs = [tile]
    out_shape = [jax.ShapeDtypeStruct((m, n), F32)]
    if emit_bf16:
        out_specs.append(tile)
        out_shape.append(jax.ShapeDtypeStruct((m, n), BF16))
    outs = pl.pallas_call(
        functools.partial(_matmul_kernel, has_res=residual is not None, has_gate=gate_e is not None,
                          emit_bf16=emit_bf16),
        grid=(m // tm, n // tn),
        in_specs=in_specs,
        out_specs=out_specs,
        out_shape=out_shape,
        compiler_params=_params("parallel", "arbitrary"),
        name=name,
    )(*args)
    return outs if emit_bf16 else outs[0]


def _group_sum_lanes(x):
    rows = lax.broadcasted_iota(jnp.int32, (LANES, LANES), 0) // HEAD_DIM
    cols = lax.broadcasted_iota(jnp.int32, (LANES, LANES), 1) // HEAD_DIM
    ones = jnp.where(rows == cols, 1.0, 0.0).astype(BF16)
    hi = x.astype(BF16)
    lo = (x - hi.astype(F32)).astype(BF16)
    return (jnp.dot(hi, ones, preferred_element_type=F32) + jnp.dot(lo, ones, preferred_element_type=F32))


def _qk_prep_kernel(x_ref, pos_ref, invf_ref, g_ref, o_ref, cos_ref, sin_ref):
    j = pl.program_id(1)
    lane = lax.broadcasted_iota(jnp.int32, x_ref.shape, 1)
    first_half = (lane % HEAD_DIM) < (HEAD_DIM // 2)

    @pl.when(j == 0)
    def _():
        ang = pos_ref[...].astype(F32) * invf_ref[...]
        cos_ref[...] = jnp.cos(ang)
        sin = jnp.sin(ang)
        sin_ref[...] = jnp.where(first_half, -sin, sin)

    x = x_ref[...]
    ms = _group_sum_lanes(x * x) * (1.0 / HEAD_DIM)
    y = x * lax.rsqrt(ms + NORM_EPS) * g_ref[0]
    partner = jnp.where(first_half, pltpu.roll(y, LANES - HEAD_DIM // 2, 1), pltpu.roll(y, HEAD_DIM // 2, 1))
    out = y * cos_ref[...] + partner * sin_ref[...]
    n_q_blocks = ATTN_WIDTH // LANES
    o_ref[...] = out * jnp.where(j < n_q_blocks, HEAD_DIM ** -0.5, 1.0)


def _qk_prep(proj_qkv, positions, q_g, k_g, *, tm=512):
    t = proj_qkv.shape[0]
    half = HEAD_DIM // 2
    inv_freq = ROPE_THETA ** (-jnp.arange(half, dtype=F32) / half)
    invf = jnp.tile(inv_freq, LANES // half).reshape(1, LANES)
    gains = jnp.stack([jnp.tile(q_g, LANES // HEAD_DIM), jnp.tile(k_g, LANES // HEAD_DIM)]).reshape(2, 1, LANES)
    n_q_blocks = ATTN_WIDTH // LANES
    return pl.pallas_call(
        _qk_prep_kernel,
        grid=(t // tm, 2 * n_q_blocks),
        in_specs=[
            pl.BlockSpec((tm, LANES), lambda i, j: (i, j)),
            pl.BlockSpec((tm, 1), lambda i, j: (i, 0)),
            pl.BlockSpec((1, LANES), lambda i, j: (0, 0)),
            pl.BlockSpec((1, 1, LANES), lambda i, j: (j // n_q_blocks, 0, 0)),
        ],
        out_specs=pl.BlockSpec((tm, LANES), lambda i, j: (i, j)),
        out_shape=jax.ShapeDtypeStruct((t, 2 * ATTN_WIDTH), F32),
        scratch_shapes=[pltpu.VMEM((tm, LANES), F32), pltpu.VMEM((tm, LANES), F32)],
        compiler_params=_params("parallel", "arbitrary"),
        name="qk_norm_rope",
    )(proj_qkv, positions.reshape(t, 1), invf, gains)


def _attn_block(q, k, v, valid, old):
    lane_q = lax.broadcasted_iota(jnp.int32, q.shape, 1)
    lane_v = lax.broadcasted_iota(jnp.int32, v.shape, 1)
    kb = k.astype(BF16)
    outs = []
    for h in range(2):
        own_q = (lane_q < HEAD_DIM) if h == 0 else (lane_q >= HEAD_DIM)
        own_v = (lane_v < HEAD_DIM) if h == 0 else (lane_v >= HEAD_DIM)
        qh = jnp.where(own_q, q, 0.0).astype(BF16)
        s = lax.dot_general(qh, kb, (((1,), (1,)), ((), ())), preferred_element_type=F32)
        s = jnp.where(valid, s, MASK_VALUE)
        mb = jnp.max(s, axis=1, keepdims=True)
        vh = jnp.where(own_v, v, 1.0).astype(BF16)
        if old is None:
            m_col = mb
            p = jnp.exp(s - m_col)
            acc = jnp.dot(p.astype(BF16), vh, preferred_element_type=F32)
        else:
            acc_old, m_old = old[h]
            alpha = jnp.exp(m_old - jnp.maximum(m_old, mb))
            m_col = jnp.maximum(m_old[:, :1], mb)
            p = jnp.exp(s - m_col)
            acc = acc_old * alpha + jnp.dot(p.astype(BF16), vh, preferred_element_type=F32)
        outs.append((acc, jnp.broadcast_to(m_col, acc.shape)))
    return outs


def _attn_kernel(q_ref, kp_ref, kc_ref, vp_ref, vc_ref, o_ref,
                 kn, vn, k4, v4, q4, acc_n, m_n, acc_d, m_d):
    blk = ATTN_BLOCK
    win = ATTN_WINDOW
    quarter = win // 4
    n_blocks = win // blk
    has_prev_window = pl.program_id(2) > 0

    kn[0:win] = kp_ref[0]
    kn[win:2 * win] = kc_ref[0]
    vn[0:win] = vp_ref[0]
    vn[win:2 * win] = vc_ref[0]
    for r in range(4):
        base = 2 * quarter * r
        k4[base:base + quarter] = kp_ref[0, pl.ds(r, quarter, stride=4), :]
        k4[base + quarter:base + 2 * quarter] = kc_ref[0, pl.ds(r, quarter, stride=4), :]
        v4[base:base + quarter] = vp_ref[0, pl.ds(r, quarter, stride=4), :]
        v4[base + quarter:base + 2 * quarter] = vc_ref[0, pl.ds(r, quarter, stride=4), :]
        q4[quarter * r:quarter * (r + 1)] = q_ref[0, pl.ds(r, quarter, stride=4), :]

    qi = lax.broadcasted_iota(jnp.int32, (blk, 2 * blk), 0)
    ki = lax.broadcasted_iota(jnp.int32, (blk, 2 * blk), 1)
    band = (ki >= qi) & (ki <= qi + blk)
    in_cur = ki >= blk

    def valid_mask(has_prev):
        return band & (in_cur | has_prev)

    def dilation1(n, carry):
        q = q_ref[0, pl.ds(pl.multiple_of(n * blk, blk), blk), :]
        start = pl.multiple_of(win + (n - 1) * blk, blk)
        k = kn[pl.ds(start, 2 * blk), :]
        v = vn[pl.ds(start, 2 * blk), :]
        outs = _attn_block(q, k, v, valid_mask(has_prev_window | (n > 0)), None)
        rows = pl.ds(pl.multiple_of(n * blk, blk), blk)
        for h in range(2):
            acc_n[h, rows, :] = outs[h][0]
            m_n[h, rows, :] = outs[h][1]
        return carry

    lax.fori_loop(0, n_blocks, dilation1, 0)

    def dilation4(idx, carry):
        r = idx // 4
        j = idx % 4
        q = q4[pl.ds(pl.multiple_of(quarter * r + blk * j, blk), blk), :]
        start = pl.multiple_of(2 * quarter * r + quarter + blk * (j - 1), blk)
        k = k4[pl.ds(start, 2 * blk), :]
        v = v4[pl.ds(start, 2 * blk), :]
        nat = pl.ds(r + 4 * blk * j, blk, stride=4)
        old = [(acc_n[h, nat, :], m_n[h, nat, :]) for h in range(2)]
        outs = _attn_block(q, k, v, valid_mask(has_prev_window | (j > 0)), old)
        rows = pl.ds(pl.multiple_of(quarter * r + blk * j, blk), blk)
        for h in range(2):
            acc_d[h, rows, :] = outs[h][0]
            m_d[h, rows, :] = outs[h][1]
        return carry

    lax.fori_loop(0, n_blocks, dilation4, 0)

    def dilation16(res, carry):
        r4 = res % 4
        c = res // 4
        rows = pl.ds(quarter * r4 + c, blk, stride=4)
        q = q4[rows, :]
        keys = pl.ds(2 * quarter * r4 + c, 2 * blk, stride=4)
        k = k4[keys, :]
        v = v4[keys, :]
        old = [(acc_d[h, rows, :], m_d[h, rows, :]) for h in range(2)]
        outs = _attn_block(q, k, v, valid_mask(has_prev_window), old)
        for h in range(2):
            acc_d[h, rows, :] = outs[h][0]
            m_d[h, rows, :] = outs[h][1]
        return carry

    lax.fori_loop(0, n_blocks, dilation16, 0)

    lane = lax.broadcasted_iota(jnp.int32, (blk, LANES), 1)
    head0 = lane < HEAD_DIM

    def finalize(idx, carry):
        r = idx // 4
        j = idx % 4
        rows = pl.ds(pl.multiple_of(quarter * r + blk * j, blk), blk)
        a0 = acc_d[0, rows, :]
        a1 = acc_d[1, rows, :]
        num = jnp.where(head0, a0, a1)
        den = pltpu.roll(jnp.where(head0, a1, a0), HEAD_DIM, 1)
        o_ref[0, pl.ds(r + 4 * blk * j, blk, stride=4), :] = num / den
        return carry

    lax.fori_loop(0, n_blocks, finalize, 0)


def _attention(qk, proj_qkv, batch, seq):
    win = ATTN_WINDOW
    n_pairs = ATTN_WIDTH // LANES
    qk3 = qk.reshape(batch, seq, 2 * ATTN_WIDTH)
    pv3 = proj_qkv.reshape(batch, seq, 3 * ATTN_WIDTH)
    blockspec = lambda fn: pl.BlockSpec((1, win, LANES), fn)
    prev = lambda w: jnp.maximum(w - 1, 0)
    scratch = [pltpu.VMEM((2 * win, LANES), F32)] * 4 + [pltpu.VMEM((win, LANES), F32)] + \
              [pltpu.VMEM((2, win, LANES), F32)] * 4
    out = pl.pallas_call(
        _attn_kernel,
        grid=(batch, n_pairs, seq // win),
        in_specs=[
            blockspec(lambda b, hp, w: (b, w, hp)),
            blockspec(lambda b, hp, w: (b, prev(w), n_pairs + hp)),
            blockspec(lambda b, hp, w: (b, w, n_pairs + hp)),
            blockspec(lambda b, hp, w: (b, prev(w), 2 * n_pairs + hp)),
            blockspec(lambda b, hp, w: (b, w, 2 * n_pairs + hp)),
        ],
        out_specs=blockspec(lambda b, hp, w: (b, w, hp)),
        out_shape=jax.ShapeDtypeStruct((batch, seq, ATTN_WIDTH), F32),
        scratch_shapes=scratch,
        compiler_params=_params("parallel", "parallel", "arbitrary"),
        name="dilated_attention",
    )(qk3, qk3, qk3, pv3, pv3)
    return out.reshape(batch * seq, ATTN_WIDTH)


def _head_allreduce(x):
    return x + pltpu.roll(x, 32, 1) + pltpu.roll(x, 64, 1) + pltpu.roll(x, 96, 1)


def _softplus(x):
    return jnp.maximum(x, 0.0) + jnp.log1p(jnp.exp(-jnp.abs(x)))


def _rwkv_prep_kernel(p_ref, halo_ref, mu_ref, w0_ref, a0_ref, kk_ref, ka_ref, wd_ref, wa_ref, wg_ref,
                      r_out, w_out, k_out, v_out, na_out, kb_out, g_out, shift, kmix, *, tm, tiles_per_seq):
    seq_start = (pl.program_id(0) % tiles_per_seq) == 0
    n_chunks = RWKV_WIDTH // LANES

    def mixed(c0):
        cols = slice(c0, c0 + LANES)
        x = p_ref[:, cols]
        shift[SUBLANES - 1:SUBLANES, :] = jnp.where(seq_start, 0.0, halo_ref[SUBLANES - 1:SUBLANES, cols])
        shift[SUBLANES:SUBLANES + tm, :] = x
        prev = shift[SUBLANES - 1:SUBLANES - 1 + tm, :]
        return x + (prev - x) * mu_ref[:, cols]

    lora0 = 3 * RWKV_WIDTH
    th = jnp.tanh(mixed(lora0)).astype(BF16)
    ad = mixed(lora0 + DECAY_LORA).astype(BF16)
    gate0 = lora0 + DECAY_LORA + ICLR_LORA
    sg = jnp.concatenate([jax.nn.sigmoid(mixed(gate0 + c * LANES)) for c in range(GATE_LORA_PAD // LANES)],
                         axis=1).astype(BF16)

    ssq = jnp.zeros((tm, LANES), F32)
    for c in range(n_chunks):
        cols = slice(c * LANES, (c + 1) * LANES)
        km = mixed(RWKV_WIDTH + c * LANES)
        kmix[:, cols] = km
        kk = km * kk_ref[:, cols]
        ssq = ssq + kk * kk
    inv_norm = 1.0 / jnp.maximum(jnp.sqrt(_head_allreduce(ssq)), 1e-12)

    for c in range(n_chunks):
        cols = slice(c * LANES, (c + 1) * LANES)
        r = mixed(c * LANES)
        v = mixed(2 * RWKV_WIDTH + c * LANES)
        k = kmix[:, cols]
        z = w0_ref[:, cols] + jnp.dot(th, wd_ref[:, cols], preferred_element_type=F32)
        w_raw = -_softplus(-z) - 0.5
        a = jax.nn.sigmoid(a0_ref[:, cols] + jnp.dot(ad, wa_ref[:, cols], preferred_element_type=F32))
        kkn = k * kk_ref[:, cols] * inv_norm
        r_out[:, cols] = r
        w_out[:, cols] = jnp.exp(-jnp.exp(w_raw))
        k_out[:, cols] = k * (1.0 + (a - 1.0) * ka_ref[:, cols])
        v_out[:, cols] = v
        na_out[:, cols] = -kkn
        kb_out[:, cols] = kkn * a
        g_out[:, cols] = jnp.dot(sg, wg_ref[:, cols], preferred_element_type=F32)


def _rwkv_prep(proj, seq, mu, w0, a0, k_k, k_a, w_decay, w_iclr, w_gate, *, tm=128):
    t, width = proj.shape
    row = lambda v: v.reshape(1, -1)
    full = lambda a: pl.BlockSpec(a.shape, lambda i: (0, 0))
    halo_blocks = tm // SUBLANES
    params = [row(mu), row(w0), row(a0), row(k_k), row(k_a), w_decay, w_iclr, w_gate]
    out_spec = pl.BlockSpec((tm, RWKV_WIDTH), lambda i: (i, 0))
    return pl.pallas_call(
        functools.partial(_rwkv_prep_kernel, tm=tm, tiles_per_seq=seq // tm),
        grid=(t // tm,),
        in_specs=[pl.BlockSpec((tm, width), lambda i: (i, 0)),
                  pl.BlockSpec((SUBLANES, width), lambda i: (jnp.maximum(i * halo_blocks - 1, 0), 0))]
                 + [full(a) for a in params],
        out_specs=[out_spec] * 7,
        out_shape=[jax.ShapeDtypeStruct((t, RWKV_WIDTH), F32)] * 7,
        scratch_shapes=[pltpu.VMEM((tm + SUBLANES, LANES), F32), pltpu.VMEM((tm, RWKV_WIDTH), F32)],
        compiler_params=_params("parallel"),
        name="rwkv_prep",
    )(proj, proj, *params)


def _rwkv_scan_kernel(r_ref, w_ref, k_ref, a_ref, b_ref, v_ref, y_ref, s_ref, *, tb):
    n_chunks = RWKV_WIDTH // LANES

    @pl.when(pl.program_id(0) == 0)
    def _():
        s_ref[...] = jnp.zeros_like(s_ref)

    def step(t, carry):
        for bi in range(s_ref.shape[0]):
            a = a_ref[bi, pl.ds(t, 1), :]
            w = w_ref[bi, pl.ds(t, 1), :]
            b = b_ref[bi, pl.ds(t, 1), :]
            k = k_ref[bi, pl.ds(t, 1), :]
            r = r_ref[bi, pl.ds(t, 1), :]
            vt = v_ref[bi, t]
            part = None
            for c in range(n_chunks):
                cols = slice(c * LANES, (c + 1) * LANES)
                term = s_ref[bi, :, cols] * a[:, cols]
                part = term if part is None else part + term
            sa = _head_allreduce(part)
            ypart = None
            for c in range(n_chunks):
                cols = slice(c * LANES, (c + 1) * LANES)
                s_new = s_ref[bi, :, cols] * w[:, cols] + sa * b[:, cols] + vt * k[:, cols]
                s_ref[bi, :, cols] = s_new
                term = s_new * r[:, cols]
                ypart = term if ypart is None else ypart + term
            y_ref[bi, t] = _head_allreduce(ypart)[:, :N_RWKV_HEADS]
        return carry

    lax.fori_loop(0, tb, step, 0)


def _rwkv_scan(r, w, k, na, kb, v, batch, seq, *, tb=64):
    shape3 = (batch, seq, RWKV_WIDTH)
    rows = [x.reshape(shape3) for x in (r, w, k, na, kb)]
    v4 = jnp.broadcast_to(v.reshape(batch, seq, HEAD_DIM, 1, N_RWKV_HEADS),
                          (batch, seq, HEAD_DIM, LANES // N_RWKV_HEADS, N_RWKV_HEADS))
    v4 = v4.reshape(batch, seq, HEAD_DIM, LANES)
    row_spec = pl.BlockSpec((batch, tb, RWKV_WIDTH), lambda i: (0, i, 0))
    y = pl.pallas_call(
        functools.partial(_rwkv_scan_kernel, tb=tb),
        grid=(seq // tb,),
        in_specs=[row_spec] * 5 + [pl.BlockSpec((batch, tb, HEAD_DIM, LANES), lambda i: (0, i, 0, 0))],
        out_specs=pl.BlockSpec((batch, tb, HEAD_DIM, N_RWKV_HEADS), lambda i: (0, i, 0, 0)),
        out_shape=jax.ShapeDtypeStruct((batch, seq, HEAD_DIM, N_RWKV_HEADS), F32),
        scratch_shapes=[pltpu.VMEM((batch, HEAD_DIM, RWKV_WIDTH), F32)],
        compiler_params=_params("arbitrary"),
        name="rwkv_scan",
    )(*rows, v4)
    return y.reshape(batch * seq, RWKV_WIDTH)


def _rwkv_post_kernel(y_ref, r_ref, k_ref, v_ref, g_ref, lnw_ref, lnb_ref, rk_ref, o_ref):
    n_chunks = RWKV_WIDTH // LANES
    chunks = [slice(c * LANES, (c + 1) * LANES) for c in range(n_chunks)]
    total = None
    bonus = None
    for cols in chunks:
        y = y_ref[:, cols]
        total = y if total is None else total + y
        term = r_ref[:, cols] * k_ref[:, cols] * rk_ref[:, cols]
        bonus = term if bonus is None else bonus + term
    mean = _head_allreduce(total) * (1.0 / HEAD_DIM)
    bonus = _head_allreduce(bonus)
    sq = None
    for cols in chunks:
        d = y_ref[:, cols] - mean
        sq = d * d if sq is None else sq + d * d
    rstd = lax.rsqrt(_head_allreduce(sq) * (1.0 / HEAD_DIM) + RWKV_GN_EPS)
    for cols in chunks:
        yn = (y_ref[:, cols] - mean) * rstd * lnw_ref[:, cols] + lnb_ref[:, cols]
        o_ref[:, cols] = ((yn + bonus * v_ref[:, cols]) * g_ref[:, cols]).astype(o_ref.dtype)


def _rwkv_post(y, r, k, v, g, ln_w, ln_b, r_k, *, tm=256):
    t = y.shape[0]
    tile = pl.BlockSpec((tm, RWKV_WIDTH), lambda i: (i, 0))
    vec = pl.BlockSpec((1, RWKV_WIDTH), lambda i: (0, 0))
    return pl.pallas_call(
        _rwkv_post_kernel,
        grid=(t // tm,),
        in_specs=[tile] * 5 + [vec] * 3,
        out_specs=tile,
        out_shape=jax.ShapeDtypeStruct((t, RWKV_WIDTH), BF16),
        compiler_params=_params("parallel"),
        name="rwkv_post",
    )(y, r, k, v, g, ln_w.reshape(1, -1), ln_b.reshape(1, -1), r_k.reshape(1, -1))


def _mlp_up_kernel(x_ref, wg_ref, wu_ref, cwg_ref, cwu_ref, cbg_ref, cbu_ref, o_ref,
                   work, carry_g, carry_u, *, tm, tiles_per_seq):
    seq_start = (pl.program_id(0) % tiles_per_seq) == 0
    j = pl.program_id(1)
    x = x_ref[...]

    def conv(w_ref, carry, cw_ref, cb_ref):
        u = jnp.dot(x, w_ref[...], preferred_element_type=F32)
        work[0:SUBLANES, :] = jnp.where(seq_start, 0.0, carry[j])
        work[SUBLANES:SUBLANES + tm, :] = u
        carry[j] = u[tm - SUBLANES:tm, :]
        lo = SUBLANES - (CONV_WIDTH - 1)
        out = work[lo:lo + tm, :] * cw_ref[0:1, :]
        out = out + work[lo + 1:lo + 1 + tm, :] * cw_ref[1:2, :]
        out = out + u * cw_ref[2:3, :]
        return out + cb_ref[...]

    gate = conv(wg_ref, carry_g, cwg_ref, cbg_ref)
    up = conv(wu_ref, carry_u, cwu_ref, cbu_ref)
    o_ref[...] = (gate * jax.nn.sigmoid(gate) * up).astype(o_ref.dtype)


def _mlp_up(x, w_gate, w_up, cw_gate, cw_up, cb_gate, cb_up, seq, *, tm=1024, tn=256):
    m, k = x.shape
    n = w_gate.shape[1]
    n_tiles = n // tn
    wspec = pl.BlockSpec((k, tn), lambda i, j: (0, j))
    cwspec = pl.BlockSpec((CONV_WIDTH, tn), lambda i, j: (0, j))
    cbspec = pl.BlockSpec((1, tn), lambda i, j: (0, j))
    return pl.pallas_call(
        functools.partial(_mlp_up_kernel, tm=tm, tiles_per_seq=seq // tm),
        grid=(m // tm, n_tiles),
        in_specs=[pl.BlockSpec((tm, k), lambda i, j: (i, 0)), wspec, wspec, cwspec, cwspec, cbspec, cbspec],
        out_specs=pl.BlockSpec((tm, tn), lambda i, j: (i, j)),
        out_shape=jax.ShapeDtypeStruct((m, n), BF16),
        scratch_shapes=[pltpu.VMEM((tm + SUBLANES, tn), F32),
                        pltpu.VMEM((n_tiles, SUBLANES, tn), F32),
                        pltpu.VMEM((n_tiles, SUBLANES, tn), F32)],
        compiler_params=_params("arbitrary", "arbitrary"),
        name="mlp_up_conv_gate",
    )(x, w_gate, w_up, cw_gate, cw_up, cb_gate.reshape(1, n), cb_up.reshape(1, n))


def _ple_embed_kernel(p_ref, w_ref, g_ref, o_ref):
    e = jnp.dot(p_ref[...].astype(BF16), w_ref[...], preferred_element_type=F32)
    ms = jnp.mean(e * e, axis=-1, keepdims=True)
    o_ref[...] = e * lax.rsqrt(ms + NORM_EPS) * g_ref[...]


def _ple_embed(p, w, g, *, tm=256):
    t, k = p.shape
    d = w.shape[1]
    return pl.pallas_call(
        _ple_embed_kernel,
        grid=(t // tm,),
        in_specs=[pl.BlockSpec((tm, k), lambda i: (i, 0)), pl.BlockSpec((k, d), lambda i: (0, 0)),
                  pl.BlockSpec((1, d), lambda i: (0, 0))],
        out_specs=pl.BlockSpec((tm, d), lambda i: (i, 0)),
        out_shape=jax.ShapeDtypeStruct((t, d), F32),
        compiler_params=_params("parallel"),
        name="ple_embed",
    )(p, w, g.reshape(1, d))


def _to_k_major(w):
    lead = w.shape[:-1]
    return w.reshape(*lead, N_RWKV_HEADS, HEAD_DIM).swapaxes(-1, -2).reshape(*lead, RWKV_WIDTH)


def _split_rwkv_cols(a, pad_value=0.0):
    pad = jnp.full(a.shape[:-1] + (GATE_LORA_PAD - GATE_LORA,), pad_value, a.dtype)
    blocks = [_to_k_major(a[..., i * RWKV_WIDTH:(i + 1) * RWKV_WIDTH]) for i in range(3)]
    return jnp.concatenate(blocks + [a[..., 3 * RWKV_WIDTH:], pad], axis=-1)


def kernel(x, p, positions, attn_norm_g, w_in, q_norm_g, k_norm_g, rwkv_mu, w0, w_decay_up, a0, w_iclr_up,
           w_gate_up, k_k, k_a, r_k, ln_x_w, ln_x_b, w_out, mlp_norm_g, w_mlp_up, conv_w, conv_b, w_mlp_down,
           w_ple_proj, ple_norm_g, w_ple_gate):
    batch, seq, d_model = x.shape
    assert w_in.shape[0] == 1 and seq % ATTN_WINDOW == 0
    t = batch * seq
    d_ff = w_mlp_down.shape[1]
    x2 = x.reshape(t, d_model)

    w_qkv = w_in[0][:, :3 * ATTN_WIDTH].astype(BF16)
    w_rwkv = _split_rwkv_cols(w_in[0][:, 3 * ATTN_WIDTH:]).astype(BF16)
    mu = _split_rwkv_cols(rwkv_mu[0])
    w_gate_lora = jnp.concatenate([_to_k_major(w_gate_up[0]),
                                   jnp.zeros((GATE_LORA_PAD - GATE_LORA, RWKV_WIDTH), F32)]).astype(BF16)
    w_out_rows = jnp.concatenate([
        w_out[0][:ATTN_WIDTH],
        w_out[0][ATTN_WIDTH:].reshape(N_RWKV_HEADS, HEAD_DIM, d_model).swapaxes(0, 1).reshape(RWKV_WIDTH, d_model),
    ]).astype(BF16)

    xn = _rmsnorm_bf16(x2, attn_norm_g[0])
    proj_qkv = _matmul(xn, w_qkv, tm=1024, tn=512, name="in_proj_attn")
    proj_rwkv = _matmul(xn, w_rwkv, tm=1024, tn=768, name="in_proj_rwkv")

    qk = _qk_prep(proj_qkv, positions, q_norm_g[0], k_norm_g[0])
    attn = _attention(qk, proj_qkv, batch, seq)

    r, w, k, v, na, kb, g = _rwkv_prep(
        proj_rwkv, seq, mu, _to_k_major(w0[0]), _to_k_major(a0[0]), _to_k_major(k_k[0]), _to_k_major(k_a[0]),
        _to_k_major(w_decay_up[0]).astype(BF16), _to_k_major(w_iclr_up[0]).astype(BF16), w_gate_lora)
    y = _rwkv_scan(r, w, k, na, kb, v, batch, seq)
    rwkv = _rwkv_post(y, r, k, v, g, _to_k_major(ln_x_w[0]), _to_k_major(ln_x_b[0]),
                      r_k[0].T.reshape(RWKV_WIDTH))

    mix = jnp.concatenate([attn.astype(BF16), rwkv], axis=1)
    h1 = _matmul(mix, w_out_rows, tm=1024, tn=512, residual=x2, name="out_proj")

    hn = _rmsnorm_bf16(h1, mlp_norm_g[0])
    w_up = w_mlp_up[0]
    act = _mlp_up(hn, w_up[:, :d_ff].astype(BF16), w_up[:, d_ff:].astype(BF16),
                  conv_w[0][:, :d_ff], conv_w[0][:, d_ff:], conv_b[0][:d_ff], conv_b[0][d_ff:], seq)
    h2, h2_bf16 = _matmul(act, w_mlp_down[0].astype(BF16), tm=512, tn=256, residual=h1, emit_bf16=True,
                          name="mlp_down")

    e = _ple_embed(p[0].reshape(t, -1), w_ple_proj[0].astype(BF16), ple_norm_g[0])
    out = _matmul(h2_bf16, w_ple_gate[0].astype(BF16), tm=1024, tn=512, residual=h2, gate_e=e,
                  name="ple_gate")
    return out.reshape(batch, seq, d_model)
```

```python
import functools

import jax
import jax.numpy as jnp
from jax import lax
from jax.experimental import pallas as pl
from jax.experimental.pallas import tpu as pltpu

F32 = jnp.float32
BF16 = jnp.bfloat16

HEAD_DIM = 64
N_ATTN_HEADS = 32
N_RWKV_HEADS = 32
ATTN_WIDTH = N_ATTN_HEADS * HEAD_DIM
RWKV_WIDTH = N_RWKV_HEADS * HEAD_DIM
ATTN_BLOCK = 128
ATTN_WINDOW = 2048
ATTN_UNROLL = 8
ROPE_THETA = 10000.0
DECAY_LORA = 128
ICLR_LORA = 128
GATE_LORA = 480
GATE_LORA_PAD = 512
CONV_WIDTH = 3
NORM_EPS = 1e-6
RWKV_GN_EPS = 64e-5
LANES = 128
SUBLANES = 8
VMEM_LIMIT = 56 * 1024 * 1024
MASK_VALUE = -1e30


def _params(*semantics):
    return pltpu.CompilerParams(dimension_semantics=semantics, vmem_limit_bytes=VMEM_LIMIT)


def _rmsnorm_kernel(x_ref, g_ref, o_ref):
    x = x_ref[...]
    ms = jnp.mean(x * x, axis=-1, keepdims=True)
    o_ref[...] = (x * lax.rsqrt(ms + NORM_EPS) * g_ref[...]).astype(o_ref.dtype)


def _rmsnorm_bf16(x, g, *, tm=256):
    m, d = x.shape
    return pl.pallas_call(
        _rmsnorm_kernel,
        grid=(m // tm,),
        in_specs=[pl.BlockSpec((tm, d), lambda i: (i, 0)), pl.BlockSpec((1, d), lambda i: (0, 0))],
        out_specs=pl.BlockSpec((tm, d), lambda i: (i, 0)),
        out_shape=jax.ShapeDtypeStruct((m, d), BF16),
        compiler_params=_params("parallel"),
        name="rmsnorm_bf16",
    )(x, g.reshape(1, d))


def _matmul_kernel(*refs, has_res, has_gate, emit_bf16):
    x_ref, w_ref = refs[0], refs[1]
    pos = 2
    res_ref = e_ref = None
    if has_res:
        res_ref = refs[pos]
        pos += 1
    if has_gate:
        e_ref = refs[pos]
        pos += 1
    o_ref = refs[pos]
    acc = jnp.dot(x_ref[...], w_ref[...], preferred_element_type=F32)
    if has_gate:
        acc = jax.nn.sigmoid(acc) * e_ref[...]
    if has_res:
        acc = res_ref[...] + acc
    o_ref[...] = acc
    if emit_bf16:
        refs[pos + 1][...] = acc.astype(BF16)


def _matmul(x, w, *, tm, tn, residual=None, gate_e=None, emit_bf16=False, name):
    m, k = x.shape
    n = w.shape[1]
    tile = pl.BlockSpec((tm, tn), lambda i, j: (i, j))
    in_specs = [pl.BlockSpec((tm, k), lambda i, j: (i, 0)), pl.BlockSpec((k, tn), lambda i, j: (0, j))]
    args = [x, w]
    if residual is not None:
        in_specs.append(tile)
        args.append(residual)
    if gate_e is not None:
        in_specs.append(tile)
        args.append(gate_e)
    out_specs = [tile]
    out_shape = [jax.ShapeDtypeStruct((m, n), F32)]
    if emit_bf16:
        out_specs.append(tile)
        out_shape.append(jax.ShapeDtypeStruct((m, n), BF16))
    outs = pl.pallas_call(
        functools.partial(_matmul_kernel, has_res=residual is not None, has_gate=gate_e is not None,
                          emit_bf16=emit_bf16),
        grid=(m // tm, n // tn),
        in_specs=in_specs,
        out_specs=out_specs,
        out_shape=out_shape,
        compiler_params=_params("parallel", "arbitrary"),
        name=name,
    )(*args)
    return outs if emit_bf16 else outs[0]


def _group_sum_lanes(x):
    rows = lax.broadcasted_iota(jnp.int32, (LANES, LANES), 0) // HEAD_DIM
    cols = lax.broadcasted_iota(jnp.int32, (LANES, LANES), 1) // HEAD_DIM
    ones = jnp.where(rows == cols, 1.0, 0.0).astype(BF16)
    hi = x.astype(BF16)
    lo = (x - hi.astype(F32)).astype(BF16)
    return (jnp.dot(hi, ones, preferred_element_type=F32) + jnp.dot(lo, ones, preferred_element_type=F32))


def _qk_prep_kernel(x_ref, pos_ref, invf_ref, g_ref, o_ref, cos_ref, sin_ref, *, n_q_blocks):
    j = pl.program_id(1)
    lane = lax.broadcasted_iota(jnp.int32, cos_ref.shape, 1)
    first_half = (lane % HEAD_DIM) < (HEAD_DIM // 2)

    @pl.when(j == 0)
    def _():
        ang = pos_ref[...].astype(F32) * invf_ref[...]
        cos_ref[...] = jnp.cos(ang)
        sin = jnp.sin(ang)
        sin_ref[...] = jnp.where(first_half, -sin, sin)

    scale = jnp.where(j < n_q_blocks, HEAD_DIM ** -0.5, 1.0)
    for c in range(x_ref.shape[1] // LANES):
        cols = slice(c * LANES, (c + 1) * LANES)
        x = x_ref[:, cols]
        ms = _group_sum_lanes(x * x) * (1.0 / HEAD_DIM)
        y = x * lax.rsqrt(ms + NORM_EPS) * g_ref[0]
        partner = jnp.where(first_half, pltpu.roll(y, LANES - HEAD_DIM // 2, 1), pltpu.roll(y, HEAD_DIM // 2, 1))
        o_ref[:, cols] = (y * cos_ref[...] + partner * sin_ref[...]) * scale


def _qk_prep(proj_qkv, positions, q_g, k_g, *, tm=512, tn=512):
    t = proj_qkv.shape[0]
    half = HEAD_DIM // 2
    inv_freq = ROPE_THETA ** (-jnp.arange(half, dtype=F32) / half)
    invf = jnp.tile(inv_freq, LANES // half).reshape(1, LANES)
    gains = jnp.stack([jnp.tile(q_g, LANES // HEAD_DIM), jnp.tile(k_g, LANES // HEAD_DIM)]).reshape(2, 1, LANES)
    tn = min(tn, ATTN_WIDTH)
    n_q_blocks = ATTN_WIDTH // tn
    return pl.pallas_call(
        functools.partial(_qk_prep_kernel, n_q_blocks=n_q_blocks),
        grid=(t // tm, 2 * n_q_blocks),
        in_specs=[
            pl.BlockSpec((tm, tn), lambda i, j: (i, j)),
            pl.BlockSpec((tm, 1), lambda i, j: (i, 0)),
            pl.BlockSpec((1, LANES), lambda i, j: (0, 0)),
            pl.BlockSpec((1, 1, LANES), lambda i, j: (j // n_q_blocks, 0, 0)),
        ],
        out_specs=pl.BlockSpec((tm, tn), lambda i, j: (i, j)),
        out_shape=jax.ShapeDtypeStruct((t, 2 * ATTN_WIDTH), F32),
        scratch_shapes=[pltpu.VMEM((tm, LANES), F32), pltpu.VMEM((tm, LANES), F32)],
        compiler_params=_params("parallel", "arbitrary"),
        name="qk_norm_rope",
    )(proj_qkv, positions.reshape(t, 1), invf, gains)


def _attn_block(q, k, v, valid, old):
    lane_q = lax.broadcasted_iota(jnp.int32, q.shape, 1)
    lane_v = lax.broadcasted_iota(jnp.int32, v.shape, 1)
    kb = k.astype(BF16)
    outs = []
    for h in range(2):
        own_q = (lane_q < HEAD_DIM) if h == 0 else (lane_q >= HEAD_DIM)
        own_v = (lane_v < HEAD_DIM) if h == 0 else (lane_v >= HEAD_DIM)
        qh = jnp.where(own_q, q, 0.0).astype(BF16)
        s = lax.dot_general(qh, kb, (((1,), (1,)), ((), ())), preferred_element_type=F32)
        s = jnp.where(valid, s, MASK_VALUE)
        mb = jnp.max(s, axis=1, keepdims=True)
        vh = jnp.where(own_v, v, 1.0).astype(BF16)
        if old is None:
            m_col = mb
            p = jnp.exp(s - m_col)
            acc = jnp.dot(p.astype(BF16), vh, preferred_element_type=F32)
        else:
            acc_old, m_old = old[h]
            alpha = jnp.exp(m_old - jnp.maximum(m_old, mb))
            m_col = jnp.maximum(m_old[:, :1], mb)
            p = jnp.exp(s - m_col)
            acc = acc_old * alpha + jnp.dot(p.astype(BF16), vh, preferred_element_type=F32)
        outs.append((acc, jnp.broadcast_to(m_col, acc.shape)))
    return outs


def _attn_kernel(q_ref, kp_ref, kc_ref, vp_ref, vc_ref, o_ref,
                 kn, vn, k4, v4, q4, acc_n, m_n, acc_d, m_d):
    blk = ATTN_BLOCK
    win = ATTN_WINDOW
    quarter = win // 4
    n_blocks = win // blk
    has_prev_window = pl.program_id(2) > 0

    kn[0:win] = kp_ref[0]
    kn[win:2 * win] = kc_ref[0]
    vn[0:win] = vp_ref[0]
    vn[win:2 * win] = vc_ref[0]
    for r in range(4):
        base = 2 * quarter * r
        k4[base:base + quarter] = kp_ref[0, pl.ds(r, quarter, stride=4), :]
        k4[base + quarter:base + 2 * quarter] = kc_ref[0, pl.ds(r, quarter, stride=4), :]
        v4[base:base + quarter] = vp_ref[0, pl.ds(r, quarter, stride=4), :]
        v4[base + quarter:base + 2 * quarter] = vc_ref[0, pl.ds(r, quarter, stride=4), :]
        q4[quarter * r:quarter * (r + 1)] = q_ref[0, pl.ds(r, quarter, stride=4), :]

    qi = lax.broadcasted_iota(jnp.int32, (blk, 2 * blk), 0)
    ki = lax.broadcasted_iota(jnp.int32, (blk, 2 * blk), 1)
    band = (ki >= qi) & (ki <= qi + blk)
    in_cur = ki >= blk

    def valid_mask(has_prev):
        return band & (in_cur | has_prev)

    def dilation1(n, carry):
        q = q_ref[0, pl.ds(pl.multiple_of(n * blk, blk), blk), :]
        start = pl.multiple_of(win + (n - 1) * blk, blk)
        k = kn[pl.ds(start, 2 * blk), :]
        v = vn[pl.ds(start, 2 * blk), :]
        outs = _attn_block(q, k, v, valid_mask(has_prev_window | (n > 0)), None)
        rows = pl.ds(pl.multiple_of(n * blk, blk), blk)
        for h in range(2):
            acc_n[h, rows, :] = outs[h][0]
            m_n[h, rows, :] = outs[h][1]
        return carry

    lax.fori_loop(0, n_blocks, dilation1, 0, unroll=ATTN_UNROLL)

    def dilation4(idx, carry):
        r = idx // 4
        j = idx % 4
        q = q4[pl.ds(pl.multiple_of(quarter * r + blk * j, blk), blk), :]
        start = pl.multiple_of(2 * quarter * r + quarter + blk * (j - 1), blk)
        k = k4[pl.ds(start, 2 * blk), :]
        v = v4[pl.ds(start, 2 * blk), :]
        nat = pl.ds(r + 4 * blk * j, blk, stride=4)
        old = [(acc_n[h, nat, :], m_n[h, nat, :]) for h in range(2)]
        outs = _attn_block(q, k, v, valid_mask(has_prev_window | (j > 0)), old)
        rows = pl.ds(pl.multiple_of(quarter * r + blk * j, blk), blk)
        for h in range(2):
            acc_d[h, rows, :] = outs[h][0]
            m_d[h, rows, :] = outs[h][1]
        return carry

    lax.fori_loop(0, n_blocks, dilation4, 0, unroll=ATTN_UNROLL)

    def dilation16(res, carry):
        r4 = res % 4
        c = res // 4
        rows = pl.ds(quarter * r4 + c, blk, stride=4)
        q = q4[rows, :]
        keys = pl.ds(2 * quarter * r4 + c, 2 * blk, stride=4)
        k = k4[keys, :]
        v = v4[keys, :]
        old = [(acc_d[h, rows, :], m_d[h, rows, :]) for h in range(2)]
        outs = _attn_block(q, k, v, valid_mask(has_prev_window), old)
        for h in range(2):
            acc_d[h, rows, :] = outs[h][0]
            m_d[h, rows, :] = outs[h][1]
        return carry

    lax.fori_loop(0, n_blocks, dilation16, 0, unroll=ATTN_UNROLL)

    lane = lax.broadcasted_iota(jnp.int32, (blk, LANES), 1)
    head0 = lane < HEAD_DIM

    def finalize(idx, carry):
        r = idx // 4
        j = idx % 4
        rows = pl.ds(pl.multiple_of(quarter * r + blk * j, blk), blk)
        a0 = acc_d[0, rows, :]
        a1 = acc_d[1, rows, :]
        num = jnp.where(head0, a0, a1)
        den = pltpu.roll(jnp.where(head0, a1, a0), HEAD_DIM, 1)
        o_ref[0, pl.ds(r + 4 * blk * j, blk, stride=4), :] = num / den
        return carry

    lax.fori_loop(0, n_blocks, finalize, 0, unroll=ATTN_UNROLL)


def _attention(qk, proj_qkv, batch, seq):
    win = ATTN_WINDOW
    n_pairs = ATTN_WIDTH // LANES
    qk3 = qk.reshape(batch, seq, 2 * ATTN_WIDTH)
    pv3 = proj_qkv.reshape(batch, seq, 3 * ATTN_WIDTH)
    blockspec = lambda fn: pl.BlockSpec((1, win, LANES), fn)
    prev = lambda w: jnp.maximum(w - 1, 0)
    scratch = [pltpu.VMEM((2 * win, LANES), F32)] * 4 + [pltpu.VMEM((win, LANES), F32)] + \
              [pltpu.VMEM((2, win, LANES), F32)] * 4
    out = pl.pallas_call(
        _attn_kernel,
        grid=(batch, n_pairs, seq // win),
        in_specs=[
            blockspec(lambda b, hp, w: (b, w, hp)),
            blockspec(lambda b, hp, w: (b, prev(w), n_pairs + hp)),
            blockspec(lambda b, hp, w: (b, w, n_pairs + hp)),
            blockspec(lambda b, hp, w: (b, prev(w), 2 * n_pairs + hp)),
            blockspec(lambda b, hp, w: (b, w, 2 * n_pairs + hp)),
        ],
        out_specs=blockspec(lambda b, hp, w: (b, w, hp)),
        out_shape=jax.ShapeDtypeStruct((batch, seq, ATTN_WIDTH), F32),
        scratch_shapes=scratch,
        compiler_params=_params("parallel", "parallel", "arbitrary"),
        name="dilated_attention",
    )(qk3, qk3, qk3, pv3, pv3)
    return out.reshape(batch * seq, ATTN_WIDTH)


def _head_allreduce(x):
    return x + pltpu.roll(x, 32, 1) + pltpu.roll(x, 64, 1) + pltpu.roll(x, 96, 1)


def _softplus(x):
    return jnp.maximum(x, 0.0) + jnp.log1p(jnp.exp(-jnp.abs(x)))


def _rwkv_prep_kernel(p_ref, halo_ref, mu_ref, w0_ref, a0_ref, kk_ref, ka_ref, wd_ref, wa_ref, wg_ref,
                      r_out, w_out, k_out, v_out, na_out, kb_out, g_out, shift, kmix, *, tm, tiles_per_seq):
    seq_start = (pl.program_id(0) % tiles_per_seq) == 0
    n_chunks = RWKV_WIDTH // LANES

    def mixed(c0):
        cols = slice(c0, c0 + LANES)
        x = p_ref[:, cols]
        shift[SUBLANES - 1:SUBLANES, :] = jnp.where(seq_start, 0.0, halo_ref[SUBLANES - 1:SUBLANES, cols])
        shift[SUBLANES:SUBLANES + tm, :] = x
        prev = shift[SUBLANES - 1:SUBLANES - 1 + tm, :]
        return x + (prev - x) * mu_ref[:, cols]

    lora0 = 3 * RWKV_WIDTH
    th = jnp.tanh(mixed(lora0)).astype(BF16)
    ad = mixed(lora0 + DECAY_LORA).astype(BF16)
    gate0 = lora0 + DECAY_LORA + ICLR_LORA
    sg = jnp.concatenate([jax.nn.sigmoid(mixed(gate0 + c * LANES)) for c in range(GATE_LORA_PAD // LANES)],
                         axis=1).astype(BF16)

    ssq = jnp.zeros((tm, LANES), F32)
    for c in range(n_chunks):
        cols = slice(c * LANES, (c + 1) * LANES)
        km = mixed(RWKV_WIDTH + c * LANES)
        kmix[:, cols] = km
        kk = km * kk_ref[:, cols]
        ssq = ssq + kk * kk
    inv_norm = 1.0 / jnp.maximum(jnp.sqrt(_head_allreduce(ssq)), 1e-12)

    for c in range(n_chunks):
        cols = slice(c * LANES, (c + 1) * LANES)
        r = mixed(c * LANES)
        v = mixed(2 * RWKV_WIDTH + c * LANES)
        k = kmix[:, cols]
        z = w0_ref[:, cols] + jnp.dot(th, wd_ref[:, cols], preferred_element_type=F32)
        w_raw = -_softplus(-z) - 0.5
        a = jax.nn.sigmoid(a0_ref[:, cols] + jnp.dot(ad, wa_ref[:, cols], preferred_element_type=F32))
        kkn = k * kk_ref[:, cols] * inv_norm
        r_out[:, cols] = r
        w_out[:, cols] = jnp.exp(-jnp.exp(w_raw))
        k_out[:, cols] = k * (1.0 + (a - 1.0) * ka_ref[:, cols])
        v_out[:, cols] = v
        na_out[:, cols] = -kkn
        kb_out[:, cols] = kkn * a
        g_out[:, cols] = jnp.dot(sg, wg_ref[:, cols], preferred_element_type=F32)


def _rwkv_prep(proj, seq, mu, w0, a0, k_k, k_a, w_decay, w_iclr, w_gate, *, tm=128):
    t, width = proj.shape
    row = lambda v: v.reshape(1, -1)
    full = lambda a: pl.BlockSpec(a.shape, lambda i: (0, 0))
    halo_blocks = tm // SUBLANES
    params = [row(mu), row(w0), row(a0), row(k_k), row(k_a), w_decay, w_iclr, w_gate]
    out_spec = pl.BlockSpec((tm, RWKV_WIDTH), lambda i: (i, 0))
    return pl.pallas_call(
        functools.partial(_rwkv_prep_kernel, tm=tm, tiles_per_seq=seq // tm),
        grid=(t // tm,),
        in_specs=[pl.BlockSpec((tm, width), lambda i: (i, 0)),
                  pl.BlockSpec((SUBLANES, width), lambda i: (jnp.maximum(i * halo_blocks - 1, 0), 0))]
                 + [full(a) for a in params],
        out_specs=[out_spec] * 7,
        out_shape=[jax.ShapeDtypeStruct((t, RWKV_WIDTH), F32)] * 7,
        scratch_shapes=[pltpu.VMEM((tm + SUBLANES, LANES), F32), pltpu.VMEM((tm, RWKV_WIDTH), F32)],
        compiler_params=_params("parallel"),
        name="rwkv_prep",
    )(proj, proj, *params)


def _rwkv_scan_kernel(r_ref, w_ref, k_ref, a_ref, b_ref, v_ref, y_ref, s_ref, *, tb):
    n_chunks = RWKV_WIDTH // LANES

    @pl.when(pl.program_id(0) == 0)
    def _():
        s_ref[...] = jnp.zeros_like(s_ref)

    def row(ref, bi, t):
        x = ref[bi, pl.ds(t, 1), :]
        return [x[:, c * LANES:(c + 1) * LANES] for c in range(n_chunks)]

    def step(t, carry):
        for bi in range(s_ref.shape[0]):
            a, w, b, k, r = (row(ref, bi, t) for ref in (a_ref, w_ref, b_ref, k_ref, r_ref))
            vt = v_ref[bi, t]
            part = None
            for c in range(n_chunks):
                cols = slice(c * LANES, (c + 1) * LANES)
                term = s_ref[bi, :, cols] * a[c]
                part = term if part is None else part + term
            sa = _head_allreduce(part)
            ypart = None
            for c in range(n_chunks):
                cols = slice(c * LANES, (c + 1) * LANES)
                s_new = s_ref[bi, :, cols] * w[c] + sa * b[c] + vt * k[c]
                s_ref[bi, :, cols] = s_new
                term = s_new * r[c]
                ypart = term if ypart is None else ypart + term
            y_ref[bi, t] = _head_allreduce(ypart)[:, :N_RWKV_HEADS]
        return carry

    lax.fori_loop(0, tb, step, 0, unroll=2)


def _rwkv_scan(r, w, k, na, kb, v, batch, seq, *, tb=64):
    shape3 = (batch, seq, RWKV_WIDTH)
    rows = [x.reshape(shape3) for x in (r, w, k, na, kb)]
    v4 = jnp.broadcast_to(v.reshape(batch, seq, HEAD_DIM, 1, N_RWKV_HEADS),
                          (batch, seq, HEAD_DIM, LANES // N_RWKV_HEADS, N_RWKV_HEADS))
    v4 = v4.reshape(batch, seq, HEAD_DIM, LANES)
    row_spec = pl.BlockSpec((batch, tb, RWKV_WIDTH), lambda i: (0, i, 0))
    y = pl.pallas_call(
        functools.partial(_rwkv_scan_kernel, tb=tb),
        grid=(seq // tb,),
        in_specs=[row_spec] * 5 + [pl.BlockSpec((batch, tb, HEAD_DIM, LANES), lambda i: (0, i, 0, 0))],
        out_specs=pl.BlockSpec((batch, tb, HEAD_DIM, N_RWKV_HEADS), lambda i: (0, i, 0, 0)),
        out_shape=jax.ShapeDtypeStruct((batch, seq, HEAD_DIM, N_RWKV_HEADS), F32),
        scratch_shapes=[pltpu.VMEM((batch, HEAD_DIM, RWKV_WIDTH), F32)],
        compiler_params=_params("arbitrary"),
        name="rwkv_scan",
    )(*rows, v4)
    return y.reshape(batch * seq, RWKV_WIDTH)


def _rwkv_post_kernel(y_ref, r_ref, k_ref, v_ref, g_ref, lnw_ref, lnb_ref, rk_ref, o_ref):
    n_chunks = RWKV_WIDTH // LANES
    chunks = [slice(c * LANES, (c + 1) * LANES) for c in range(n_chunks)]
    total = None
    bonus = None
    for cols in chunks:
        y = y_ref[:, cols]
        total = y if total is None else total + y
        term = r_ref[:, cols] * k_ref[:, cols] * rk_ref[:, cols]
        bonus = term if bonus is None else bonus + term
    mean = _head_allreduce(total) * (1.0 / HEAD_DIM)
    bonus = _head_allreduce(bonus)
    sq = None
    for cols in chunks:
        d = y_ref[:, cols] - mean
        sq = d * d if sq is None else sq + d * d
    rstd = lax.rsqrt(_head_allreduce(sq) * (1.0 / HEAD_DIM) + RWKV_GN_EPS)
    for cols in chunks:
        yn = (y_ref[:, cols] - mean) * rstd * lnw_ref[:, cols] + lnb_ref[:, cols]
        o_ref[:, cols] = ((yn + bonus * v_ref[:, cols]) * g_ref[:, cols]).astype(o_ref.dtype)


def _rwkv_post(y, r, k, v, g, ln_w, ln_b, r_k, *, tm=256):
    t = y.shape[0]
    tile = pl.BlockSpec((tm, RWKV_WIDTH), lambda i: (i, 0))
    vec = pl.BlockSpec((1, RWKV_WIDTH), lambda i: (0, 0))
    return pl.pallas_call(
        _rwkv_post_kernel,
        grid=(t // tm,),
        in_specs=[tile] * 5 + [vec] * 3,
        out_specs=tile,
        out_shape=jax.ShapeDtypeStruct((t, RWKV_WIDTH), BF16),
        compiler_params=_params("parallel"),
        name="rwkv_post",
    )(y, r, k, v, g, ln_w.reshape(1, -1), ln_b.reshape(1, -1), r_k.reshape(1, -1))


def _mlp_up_kernel(x_ref, wg_ref, wu_ref, cwg_ref, cwu_ref, cbg_ref, cbu_ref, o_ref,
                   work, carry_g, carry_u, *, tm, tiles_per_seq):
    seq_start = (pl.program_id(0) % tiles_per_seq) == 0
    j = pl.program_id(1)
    x = x_ref[...]

    def conv(w_ref, carry, cw_ref, cb_ref):
        u = jnp.dot(x, w_ref[...], preferred_element_type=F32)
        work[0:SUBLANES, :] = jnp.where(seq_start, 0.0, carry[j])
        work[SUBLANES:SUBLANES + tm, :] = u
        carry[j] = u[tm - SUBLANES:tm, :]
        lo = SUBLANES - (CONV_WIDTH - 1)
        out = work[lo:lo + tm, :] * cw_ref[0:1, :]
        out = out + work[lo + 1:lo + 1 + tm, :] * cw_ref[1:2, :]
        out = out + u * cw_ref[2:3, :]
        return out + cb_ref[...]

    gate = conv(wg_ref, carry_g, cwg_ref, cbg_ref)
    up = conv(wu_ref, carry_u, cwu_ref, cbu_ref)
    o_ref[...] = (gate * jax.nn.sigmoid(gate) * up).astype(o_ref.dtype)


def _mlp_up(x, w_gate, w_up, cw_gate, cw_up, cb_gate, cb_up, seq, *, tm=1024, tn=256):
    m, k = x.shape
    n = w_gate.shape[1]
    n_tiles = n // tn
    wspec = pl.BlockSpec((k, tn), lambda i, j: (0, j))
    cwspec = pl.BlockSpec((CONV_WIDTH, tn), lambda i, j: (0, j))
    cbspec = pl.BlockSpec((1, tn), lambda i, j: (0, j))
    return pl.pallas_call(
        functools.partial(_mlp_up_kernel, tm=tm, tiles_per_seq=seq // tm),
        grid=(m // tm, n_tiles),
        in_specs=[pl.BlockSpec((tm, k), lambda i, j: (i, 0)), wspec, wspec, cwspec, cwspec, cbspec, cbspec],
        out_specs=pl.BlockSpec((tm, tn), lambda i, j: (i, j)),
        out_shape=jax.ShapeDtypeStruct((m, n), BF16),
        scratch_shapes=[pltpu.VMEM((tm + SUBLANES, tn), F32),
                        pltpu.VMEM((n_tiles, SUBLANES, tn), F32),
                        pltpu.VMEM((n_tiles, SUBLANES, tn), F32)],
        compiler_params=_params("arbitrary", "arbitrary"),
        name="mlp_up_conv_gate",
    )(x, w_gate, w_up, cw_gate, cw_up, cb_gate.reshape(1, n), cb_up.reshape(1, n))


def _ple_embed_kernel(p_ref, w_ref, g_ref, o_ref):
    e = jnp.dot(p_ref[...].astype(BF16), w_ref[...], preferred_element_type=F32)
    ms = jnp.mean(e * e, axis=-1, keepdims=True)
    o_ref[...] = e * lax.rsqrt(ms + NORM_EPS) * g_ref[...]


def _ple_embed(p, w, g, *, tm=256):
    t, k = p.shape
    d = w.shape[1]
    return pl.pallas_call(
        _ple_embed_kernel,
        grid=(t // tm,),
        in_specs=[pl.BlockSpec((tm, k), lambda i: (i, 0)), pl.BlockSpec((k, d), lambda i: (0, 0)),
                  pl.BlockSpec((1, d), lambda i: (0, 0))],
        out_specs=pl.BlockSpec((tm, d), lambda i: (i, 0)),
        out_shape=jax.ShapeDtypeStruct((t, d), F32),
        compiler_params=_params("parallel"),
        name="ple_embed",
    )(p, w, g.reshape(1, d))


def _to_k_major(w):
    lead = w.shape[:-1]
    return w.reshape(*lead, N_RWKV_HEADS, HEAD_DIM).swapaxes(-1, -2).reshape(*lead, RWKV_WIDTH)


def _split_rwkv_cols(a, pad_value=0.0):
    pad = jnp.full(a.shape[:-1] + (GATE_LORA_PAD - GATE_LORA,), pad_value, a.dtype)
    blocks = [_to_k_major(a[..., i * RWKV_WIDTH:(i + 1) * RWKV_WIDTH]) for i in range(3)]
    return jnp.concatenate(blocks + [a[..., 3 * RWKV_WIDTH:], pad], axis=-1)


def kernel(x, p, positions, attn_norm_g, w_in, q_norm_g, k_norm_g, rwkv_mu, w0, w_decay_up, a0, w_iclr_up,
           w_gate_up, k_k, k_a, r_k, ln_x_w, ln_x_b, w_out, mlp_norm_g, w_mlp_up, conv_w, conv_b, w_mlp_down,
           w_ple_proj, ple_norm_g, w_ple_gate):
    batch, seq, d_model = x.shape
    assert w_in.shape[0] == 1 and seq % ATTN_WINDOW == 0
    t = batch * seq
    d_ff = w_mlp_down.shape[1]
    x2 = x.reshape(t, d_model)

    w_qkv = w_in[0][:, :3 * ATTN_WIDTH].astype(BF16)
    w_rwkv = _split_rwkv_cols(w_in[0][:, 3 * ATTN_WIDTH:]).astype(BF16)
    mu = _split_rwkv_cols(rwkv_mu[0])
    w_gate_lora = jnp.concatenate([_to_k_major(w_gate_up[0]),
                                   jnp.zeros((GATE_LORA_PAD - GATE_LORA, RWKV_WIDTH), F32)]).astype(BF16)
    w_out_rows = jnp.concatenate([
        w_out[0][:ATTN_WIDTH],
        w_out[0][ATTN_WIDTH:].reshape(N_RWKV_HEADS, HEAD_DIM, d_model).swapaxes(0, 1).reshape(RWKV_WIDTH, d_model),
    ]).astype(BF16)

    xn = _rmsnorm_bf16(x2, attn_norm_g[0])
    proj_qkv = _matmul(xn, w_qkv, tm=1024, tn=512, name="in_proj_attn")
    proj_rwkv = _matmul(xn, w_rwkv, tm=1024, tn=768, name="in_proj_rwkv")

    qk = _qk_prep(proj_qkv, positions, q_norm_g[0], k_norm_g[0])
    attn = _attention(qk, proj_qkv, batch, seq)

    r, w, k, v, na, kb, g = _rwkv_prep(
        proj_rwkv, seq, mu, _to_k_major(w0[0]), _to_k_major(a0[0]), _to_k_major(k_k[0]), _to_k_major(k_a[0]),
        _to_k_major(w_decay_up[0]).astype(BF16), _to_k_major(w_iclr_up[0]).astype(BF16), w_gate_lora)
    y = _rwkv_scan(r, w, k, na, kb, v, batch, seq)
    rwkv = _rwkv_post(y, r, k, v, g, _to_k_major(ln_x_w[0]), _to_k_major(ln_x_b[0]),
                      r_k[0].T.reshape(RWKV_WIDTH))

    mix = jnp.concatenate([attn.astype(BF16), rwkv], axis=1)
    h1 = _matmul(mix, w_out_rows, tm=1024, tn=512, residual=x2, name="out_proj")

    hn = _rmsnorm_bf16(h1, mlp_norm_g[0])
    w_up = w_mlp_up[0]
    act = _mlp_up(hn, w_up[:, :d_ff].astype(BF16), w_up[:, d_ff:].astype(BF16),
                  conv_w[0][:, :d_ff], conv_w[0][:, d_ff:], conv_b[0][:d_ff], conv_b[0][d_ff:], seq)
    h2, h2_bf16 = _matmul(act, w_mlp_down[0].astype(BF16), tm=512, tn=256, residual=h1, emit_bf16=True,
                          name="mlp_down")

    e = _ple_embed(p[0].reshape(t, -1), w_ple_proj[0].astype(BF16), ple_norm_g[0])
    out = _matmul(h2_bf16, w_ple_gate[0].astype(BF16), tm=1024, tn=512, residual=h2, gate_e=e,
                  name="ple_gate")
    return out.reshape(batch, seq, d_model)
```

```python
import functools

import jax
import jax.numpy as jnp
from jax import lax
from jax.experimental import pallas as pl
from jax.experimental.pallas import tpu as pltpu

F32 = jnp.float32
BF16 = jnp.bfloat16

HEAD_DIM = 64
N_ATTN_HEADS = 32
N_RWKV_HEADS = 32
ATTN_WIDTH = N_ATTN_HEADS * HEAD_DIM
RWKV_WIDTH = N_RWKV_HEADS * HEAD_DIM
ATTN_BLOCK = 128
ATTN_WINDOW = 2048
ATTN_UNROLL = 8
ROPE_THETA = 10000.0
DECAY_LORA = 128
ICLR_LORA = 128
GATE_LORA = 480
GATE_LORA_PAD = 512
RWKV_TILE = 128
CONV_WIDTH = 3
NORM_EPS = 1e-6
RWKV_GN_EPS = 64e-5
LANES = 128
SUBLANES = 8
VMEM_LIMIT = 56 * 1024 * 1024
MASK_VALUE = -1e30


def _params(*semantics):
    return pltpu.CompilerParams(dimension_semantics=semantics, vmem_limit_bytes=VMEM_LIMIT)


def _rmsnorm_kernel(x_ref, g_ref, o_ref):
    x = x_ref[...]
    ms = jnp.mean(x * x, axis=-1, keepdims=True)
    o_ref[...] = (x * lax.rsqrt(ms + NORM_EPS) * g_ref[...]).astype(o_ref.dtype)


def _rmsnorm_bf16(x, g, *, tm=256):
    m, d = x.shape
    return pl.pallas_call(
        _rmsnorm_kernel,
        grid=(m // tm,),
        in_specs=[pl.BlockSpec((tm, d), lambda i: (i, 0)), pl.BlockSpec((1, d), lambda i: (0, 0))],
        out_specs=pl.BlockSpec((tm, d), lambda i: (i, 0)),
        out_shape=jax.ShapeDtypeStruct((m, d), BF16),
        compiler_params=_params("parallel"),
        name="rmsnorm_bf16",
    )(x, g.reshape(1, d))


def _matmul_kernel(*refs, has_res, has_gate, emit_bf16):
    x_ref, w_ref = refs[0], refs[1]
    pos = 2
    res_ref = e_ref = None
    if has_res:
        res_ref = refs[pos]
        pos += 1
    if has_gate:
        e_ref = refs[pos]
        pos += 1
    o_ref = refs[pos]
    acc = jnp.dot(x_ref[...], w_ref[...], preferred_element_type=F32)
    if has_gate:
        acc = jax.nn.sigmoid(acc) * e_ref[...]
    if has_res:
        acc = res_ref[...] + acc
    o_ref[...] = acc
    if emit_bf16:
        refs[pos + 1][...] = acc.astype(BF16)


def _matmul(x, w, *, tm, tn, residual=None, gate_e=None, emit_bf16=False, name):
    m, k = x.shape
    n = w.shape[1]
    tile = pl.BlockSpec((tm, tn), lambda i, j: (i, j))
    in_specs = [pl.BlockSpec((tm, k), lambda i, j: (i, 0)), pl.BlockSpec((k, tn), lambda i, j: (0, j))]
    args = [x, w]
    if residual is not None:
        in_specs.append(tile)
        args.append(residual)
    if gate_e is not None:
        in_specs.append(tile)
        args.append(gate_e)
    out_specs = [tile]
    out_shape = [jax.ShapeDtypeStruct((m, n), F32)]
    if emit_bf16:
        out_specs.append(tile)
        out_shape.append(jax.ShapeDtypeStruct((m, n), BF16))
    outs = pl.pallas_call(
        functools.partial(_matmul_kernel, has_res=residual is not None, has_gate=gate_e is not None,
                          emit_bf16=emit_bf16),
        grid=(m // tm, n // tn),
        in_specs=in_specs,
        out_specs=out_specs,
        out_shape=out_shape,
        compiler_params=_params("parallel", "arbitrary"),
        name=name,
    )(*args)
    return outs if emit_bf16 else outs[0]


def _out_proj_kernel(a_ref, r_ref, wa_ref, wr_ref, res_ref, o_ref, a_bf16):
    @pl.when(pl.program_id(1) == 0)
    def _():
        a_bf16[...] = a_ref[...].astype(BF16)

    acc = jnp.dot(a_bf16[...], wa_ref[...], preferred_element_type=F32)
    acc = acc + jnp.dot(r_ref[...], wr_ref[...], preferred_element_type=F32)
    o_ref[...] = res_ref[...] + acc


def _out_proj(attn, rwkv, w, residual, *, tm=1024, tn=512):
    m, ka = attn.shape
    kr = rwkv.shape[1]
    n = w.shape[1]
    assert ka == kr
    tile = pl.BlockSpec((tm, tn), lambda i, j: (i, j))
    return pl.pallas_call(
        _out_proj_kernel,
        grid=(m // tm, n // tn),
        in_specs=[pl.BlockSpec((tm, ka), lambda i, j: (i, 0)), pl.BlockSpec((tm, kr), lambda i, j: (i, 0)),
                  pl.BlockSpec((ka, tn), lambda i, j: (0, j)), pl.BlockSpec((kr, tn), lambda i, j: (1, j)),
                  tile],
        out_specs=tile,
        out_shape=jax.ShapeDtypeStruct((m, n), F32),
        scratch_shapes=[pltpu.VMEM((tm, ka), BF16)],
        compiler_params=_params("parallel", "arbitrary"),
        name="out_proj",
    )(attn, rwkv, w, w, residual)


def _group_sum_lanes(x):
    rows = lax.broadcasted_iota(jnp.int32, (LANES, LANES), 0) // HEAD_DIM
    cols = lax.broadcasted_iota(jnp.int32, (LANES, LANES), 1) // HEAD_DIM
    ones = jnp.where(rows == cols, 1.0, 0.0).astype(BF16)
    hi = x.astype(BF16)
    lo = (x - hi.astype(F32)).astype(BF16)
    return (jnp.dot(hi, ones, preferred_element_type=F32) + jnp.dot(lo, ones, preferred_element_type=F32))


def _qk_prep_kernel(x_ref, pos_ref, invf_ref, g_ref, o_ref, cos_ref, sin_ref, *, n_q_blocks):
    j = pl.program_id(1)
    lane = lax.broadcasted_iota(jnp.int32, cos_ref.shape, 1)
    first_half = (lane % HEAD_DIM) < (HEAD_DIM // 2)

    @pl.when(j == 0)
    def _():
        ang = pos_ref[...].astype(F32) * invf_ref[...]
        cos_ref[...] = jnp.cos(ang)
        sin = jnp.sin(ang)
        sin_ref[...] = jnp.where(first_half, -sin, sin)

    scale = jnp.where(j < n_q_blocks, HEAD_DIM ** -0.5, 1.0)
    for c in range(x_ref.shape[1] // LANES):
        cols = slice(c * LANES, (c + 1) * LANES)
        x = x_ref[:, cols]
        ms = _group_sum_lanes(x * x) * (1.0 / HEAD_DIM)
        y = x * lax.rsqrt(ms + NORM_EPS) * g_ref[0]
        partner = jnp.where(first_half, pltpu.roll(y, LANES - HEAD_DIM // 2, 1), pltpu.roll(y, HEAD_DIM // 2, 1))
        o_ref[:, cols] = (y * cos_ref[...] + partner * sin_ref[...]) * scale


def _qk_prep(proj_qkv, positions, q_g, k_g, *, tm=512, tn=512):
    t = proj_qkv.shape[0]
    half = HEAD_DIM // 2
    inv_freq = ROPE_THETA ** (-jnp.arange(half, dtype=F32) / half)
    invf = jnp.tile(inv_freq, LANES // half).reshape(1, LANES)
    gains = jnp.stack([jnp.tile(q_g, LANES // HEAD_DIM), jnp.tile(k_g, LANES // HEAD_DIM)]).reshape(2, 1, LANES)
    tn = min(tn, ATTN_WIDTH)
    n_q_blocks = ATTN_WIDTH // tn
    return pl.pallas_call(
        functools.partial(_qk_prep_kernel, n_q_blocks=n_q_blocks),
        grid=(t // tm, 2 * n_q_blocks),
        in_specs=[
            pl.BlockSpec((tm, tn), lambda i, j: (i, j)),
            pl.BlockSpec((tm, 1), lambda i, j: (i, 0)),
            pl.BlockSpec((1, LANES), lambda i, j: (0, 0)),
            pl.BlockSpec((1, 1, LANES), lambda i, j: (j // n_q_blocks, 0, 0)),
        ],
        out_specs=pl.BlockSpec((tm, tn), lambda i, j: (i, j)),
        out_shape=jax.ShapeDtypeStruct((t, 2 * ATTN_WIDTH), F32),
        scratch_shapes=[pltpu.VMEM((tm, LANES), F32), pltpu.VMEM((tm, LANES), F32)],
        compiler_params=_params("parallel", "arbitrary"),
        name="qk_norm_rope",
    )(proj_qkv, positions.reshape(t, 1), invf, gains)


def _attn_block(q, k, v, valid, old):
    lane_q = lax.broadcasted_iota(jnp.int32, q.shape, 1)
    lane_v = lax.broadcasted_iota(jnp.int32, v.shape, 1)
    kb = k.astype(BF16)
    outs = []
    for h in range(2):
        own_q = (lane_q < HEAD_DIM) if h == 0 else (lane_q >= HEAD_DIM)
        own_v = (lane_v < HEAD_DIM) if h == 0 else (lane_v >= HEAD_DIM)
        qh = jnp.where(own_q, q, 0.0).astype(BF16)
        s = lax.dot_general(qh, kb, (((1,), (1,)), ((), ())), preferred_element_type=F32)
        s = jnp.where(valid, s, MASK_VALUE)
        mb = jnp.max(s, axis=1, keepdims=True)
        vh = jnp.where(own_v, v, 1.0).astype(BF16)
        if old is None:
            m_new = jnp.broadcast_to(mb, q.shape)
            p = jnp.exp(s - mb)
            acc = jnp.dot(p.astype(BF16), vh, preferred_element_type=F32)
        else:
            acc_old, m_old = old[h]
            m_new = jnp.maximum(m_old, mb)
            alpha = jnp.exp(m_old - m_new)
            p = jnp.exp(s - jnp.concatenate([m_new, m_new], axis=1))
            acc = acc_old * alpha + jnp.dot(p.astype(BF16), vh, preferred_element_type=F32)
        outs.append((acc, m_new))
    return outs


def _attn_kernel(q_ref, kp_ref, kc_ref, vp_ref, vc_ref, o_ref,
                 kn, vn, k4, v4, q4, acc_n, m_n, acc_d, m_d):
    blk = ATTN_BLOCK
    win = ATTN_WINDOW
    quarter = win // 4
    n_blocks = win // blk
    has_prev_window = pl.program_id(2) > 0

    kn[0:win] = kp_ref[0]
    kn[win:2 * win] = kc_ref[0]
    vn[0:win] = vp_ref[0]
    vn[win:2 * win] = vc_ref[0]
    for r in range(4):
        base = 2 * quarter * r
        k4[base:base + quarter] = kp_ref[0, pl.ds(r, quarter, stride=4), :]
        k4[base + quarter:base + 2 * quarter] = kc_ref[0, pl.ds(r, quarter, stride=4), :]
        v4[base:base + quarter] = vp_ref[0, pl.ds(r, quarter, stride=4), :]
        v4[base + quarter:base + 2 * quarter] = vc_ref[0, pl.ds(r, quarter, stride=4), :]
        q4[quarter * r:quarter * (r + 1)] = q_ref[0, pl.ds(r, quarter, stride=4), :]

    qi = lax.broadcasted_iota(jnp.int32, (blk, 2 * blk), 0)
    ki = lax.broadcasted_iota(jnp.int32, (blk, 2 * blk), 1)
    band = (ki >= qi) & (ki <= qi + blk)
    in_cur = ki >= blk

    def valid_mask(has_prev):
        return band & (in_cur | has_prev)

    def dilation1(n, carry):
        q = q_ref[0, pl.ds(pl.multiple_of(n * blk, blk), blk), :]
        start = pl.multiple_of(win + (n - 1) * blk, blk)
        k = kn[pl.ds(start, 2 * blk), :]
        v = vn[pl.ds(start, 2 * blk), :]
        outs = _attn_block(q, k, v, valid_mask(has_prev_window | (n > 0)), None)
        rows = pl.ds(pl.multiple_of(n * blk, blk), blk)
        for h in range(2):
            acc_n[h, rows, :] = outs[h][0]
            m_n[h, rows, :] = outs[h][1]
        return carry

    lax.fori_loop(0, n_blocks, dilation1, 0, unroll=ATTN_UNROLL)

    def dilation4(idx, carry):
        r = idx // 4
        j = idx % 4
        q = q4[pl.ds(pl.multiple_of(quarter * r + blk * j, blk), blk), :]
        start = pl.multiple_of(2 * quarter * r + quarter + blk * (j - 1), blk)
        k = k4[pl.ds(start, 2 * blk), :]
        v = v4[pl.ds(start, 2 * blk), :]
        nat = pl.ds(r + 4 * blk * j, blk, stride=4)
        old = [(acc_n[h, nat, :], m_n[h, nat, :]) for h in range(2)]
        outs = _attn_block(q, k, v, valid_mask(has_prev_window | (j > 0)), old)
        rows = pl.ds(pl.multiple_of(quarter * r + blk * j, blk), blk)
        for h in range(2):
            acc_d[h, rows, :] = outs[h][0]
            m_d[h, rows, :] = outs[h][1]
        return carry

    lax.fori_loop(0, n_blocks, dilation4, 0, unroll=ATTN_UNROLL)

    def dilation16(res, carry):
        r4 = res % 4
        c = res // 4
        rows = pl.ds(quarter * r4 + c, blk, stride=4)
        q = q4[rows, :]
        keys = pl.ds(2 * quarter * r4 + c, 2 * blk, stride=4)
        k = k4[keys, :]
        v = v4[keys, :]
        old = [(acc_d[h, rows, :], m_d[h, rows, :]) for h in range(2)]
        outs = _attn_block(q, k, v, valid_mask(has_prev_window), old)
        for h in range(2):
            acc_d[h, rows, :] = outs[h][0]
            m_d[h, rows, :] = outs[h][1]
        return carry

    lax.fori_loop(0, n_blocks, dilation16, 0, unroll=ATTN_UNROLL)

    lane = lax.broadcasted_iota(jnp.int32, (blk, LANES), 1)
    head0 = lane < HEAD_DIM

    def finalize(idx, carry):
        r = idx // 4
        j = idx % 4
        rows = pl.ds(pl.multiple_of(quarter * r + blk * j, blk), blk)
        a0 = acc_d[0, rows, :]
        a1 = acc_d[1, rows, :]
        num = jnp.where(head0, a0, a1)
        den = pltpu.roll(jnp.where(head0, a1, a0), HEAD_DIM, 1)
        o_ref[0, pl.ds(r + 4 * blk * j, blk, stride=4), :] = num / den
        return carry

    lax.fori_loop(0, n_blocks, finalize, 0, unroll=ATTN_UNROLL)


def _attention(qk, proj_qkv, batch, seq):
    win = ATTN_WINDOW
    n_pairs = ATTN_WIDTH // LANES
    qk3 = qk.reshape(batch, seq, 2 * ATTN_WIDTH)
    pv3 = proj_qkv.reshape(batch, seq, 3 * ATTN_WIDTH)
    blockspec = lambda fn: pl.BlockSpec((1, win, LANES), fn)
    prev = lambda w: jnp.maximum(w - 1, 0)
    scratch = [pltpu.VMEM((2 * win, LANES), F32)] * 4 + [pltpu.VMEM((win, LANES), F32)] + \
              [pltpu.VMEM((2, win, LANES), F32)] * 4
    out = pl.pallas_call(
        _attn_kernel,
        grid=(batch, n_pairs, seq // win),
        in_specs=[
            blockspec(lambda b, hp, w: (b, w, hp)),
            blockspec(lambda b, hp, w: (b, prev(w), n_pairs + hp)),
            blockspec(lambda b, hp, w: (b, w, n_pairs + hp)),
            blockspec(lambda b, hp, w: (b, prev(w), 2 * n_pairs + hp)),
            blockspec(lambda b, hp, w: (b, w, 2 * n_pairs + hp)),
        ],
        out_specs=blockspec(lambda b, hp, w: (b, w, hp)),
        out_shape=jax.ShapeDtypeStruct((batch, seq, ATTN_WIDTH), F32),
        scratch_shapes=scratch,
        compiler_params=_params("parallel", "parallel", "arbitrary"),
        name="dilated_attention",
    )(qk3, qk3, qk3, pv3, pv3)
    return out.reshape(batch * seq, ATTN_WIDTH)


def _head_allreduce(x):
    return x + pltpu.roll(x, 32, 1) + pltpu.roll(x, 64, 1) + pltpu.roll(x, 96, 1)


def _softplus(x):
    return jnp.maximum(x, 0.0) + jnp.log1p(jnp.exp(-jnp.abs(x)))


def _rwkv_prep_kernel(p_ref, halo_ref, mu_ref, w0_ref, a0_ref, kk_ref, ka_ref, wd_ref, wa_ref, wg_ref,
                      r_out, w_out, k_out, v_out, na_out, kb_out, g_out, shift, kmix, *, tm, tiles_per_seq):
    seq_start = (pl.program_id(0) % tiles_per_seq) == 0
    n_chunks = RWKV_WIDTH // LANES

    def mixed(c0):
        cols = slice(c0, c0 + LANES)
        x = p_ref[:, cols]
        shift[SUBLANES - 1:SUBLANES, :] = jnp.where(seq_start, 0.0, halo_ref[SUBLANES - 1:SUBLANES, cols])
        shift[SUBLANES:SUBLANES + tm, :] = x
        prev = shift[SUBLANES - 1:SUBLANES - 1 + tm, :]
        return x + (prev - x) * mu_ref[:, cols]

    lora0 = 3 * RWKV_WIDTH
    th = jnp.tanh(mixed(lora0)).astype(BF16)
    ad = mixed(lora0 + DECAY_LORA).astype(BF16)
    gate0 = lora0 + DECAY_LORA + ICLR_LORA
    sg = jnp.concatenate([jax.nn.sigmoid(mixed(gate0 + c * LANES)) for c in range(GATE_LORA_PAD // LANES)],
                         axis=1).astype(BF16)

    ssq = jnp.zeros((tm, LANES), F32)
    for c in range(n_chunks):
        cols = slice(c * LANES, (c + 1) * LANES)
        km = mixed(RWKV_WIDTH + c * LANES)
        kmix[:, cols] = km
        kk = km * kk_ref[:, cols]
        ssq = ssq + kk * kk
    inv_norm = 1.0 / jnp.maximum(jnp.sqrt(_head_allreduce(ssq)), 1e-12)

    for c in range(n_chunks):
        cols = slice(c * LANES, (c + 1) * LANES)
        rows = slice(c * tm, (c + 1) * tm)
        r = mixed(c * LANES)
        v = mixed(2 * RWKV_WIDTH + c * LANES)
        k = kmix[:, cols]
        z = w0_ref[:, cols] + jnp.dot(th, wd_ref[:, cols], preferred_element_type=F32)
        w_raw = -_softplus(-z) - 0.5
        a = jax.nn.sigmoid(a0_ref[:, cols] + jnp.dot(ad, wa_ref[:, cols], preferred_element_type=F32))
        kkn = k * kk_ref[:, cols] * inv_norm
        r_out[0, rows, :] = r
        w_out[0, rows, :] = jnp.exp(-jnp.exp(w_raw))
        k_out[0, rows, :] = k * (1.0 + (a - 1.0) * ka_ref[:, cols])
        v_out[0, rows, :] = v
        na_out[0, rows, :] = -kkn
        kb_out[0, rows, :] = kkn * a
        g_out[0, rows, :] = jnp.dot(sg, wg_ref[:, cols], preferred_element_type=F32)


def _rwkv_prep(proj, seq, mu, w0, a0, k_k, k_a, w_decay, w_iclr, w_gate):
    t, width = proj.shape
    tm = RWKV_TILE
    n_chunks = RWKV_WIDTH // LANES
    row = lambda v: v.reshape(1, -1)
    full = lambda a: pl.BlockSpec(a.shape, lambda i: (0, 0))
    halo_blocks = tm // SUBLANES
    params = [row(mu), row(w0), row(a0), row(k_k), row(k_a), w_decay, w_iclr, w_gate]
    out_spec = pl.BlockSpec((1, n_chunks * tm, LANES), lambda i: (i, 0, 0))
    return pl.pallas_call(
        functools.partial(_rwkv_prep_kernel, tm=tm, tiles_per_seq=seq // tm),
        grid=(t // tm,),
        in_specs=[pl.BlockSpec((tm, width), lambda i: (i, 0)),
                  pl.BlockSpec((SUBLANES, width), lambda i: (jnp.maximum(i * halo_blocks - 1, 0), 0))]
                 + [full(a) for a in params],
        out_specs=[out_spec] * 7,
        out_shape=[jax.ShapeDtypeStruct((t // tm, n_chunks * tm, LANES), F32)] * 7,
        scratch_shapes=[pltpu.VMEM((tm + SUBLANES, LANES), F32), pltpu.VMEM((tm, RWKV_WIDTH), F32)],
        compiler_params=_params("parallel"),
        name="rwkv_prep",
    )(proj, proj, *params)


def _rwkv_scan_kernel(r_ref, w_ref, k_ref, a_ref, b_ref, v_ref, y_ref, s_ref, *, tm):
    n_chunks = RWKV_WIDTH // LANES
    n_groups = LANES // N_RWKV_HEADS
    lane_group = lax.broadcasted_iota(jnp.int32, (SUBLANES, LANES), 1) // N_RWKV_HEADS

    @pl.when(pl.program_id(0) == 0)
    def _():
        s_ref[...] = jnp.zeros_like(s_ref)

    def row(ref, bi, t, c):
        return ref[bi, 0, pl.ds(c * tm + t, 1), :]

    def pick_groups(tiles):
        out = tiles[n_groups - 1]
        for g in range(n_groups - 2, -1, -1):
            out = jnp.where(lane_group == g, tiles[g], out)
        return out

    def step(t, carry):
        for bi in range(s_ref.shape[0]):
            v_rows = {}
            for half in range(2):
                x = v_ref[bi, 0, pl.ds(SUBLANES * half * tm + t, SUBLANES, stride=tm), :]
                rolled = [x] + [pltpu.roll(x, N_RWKV_HEADS * j, 1) for j in range(1, n_groups)]
                for q in range(n_groups):
                    v_rows[(q, half)] = pick_groups([rolled[(g - q) % n_groups] for g in range(n_groups)])
            vt = jnp.concatenate([v_rows[(q, half)] for q in range(n_groups) for half in range(2)], axis=0)
            part = None
            for c in range(n_chunks):
                cols = slice(c * LANES, (c + 1) * LANES)
                term = s_ref[bi, :, cols] * row(a_ref, bi, t, c)
                part = term if part is None else part + term
            sa = _head_allreduce(part)
            ypart = None
            for c in range(n_chunks):
                cols = slice(c * LANES, (c + 1) * LANES)
                s_new = (s_ref[bi, :, cols] * row(w_ref, bi, t, c) + sa * row(b_ref, bi, t, c)
                         + vt * row(k_ref, bi, t, c))
                s_ref[bi, :, cols] = s_new
                term = s_new * row(r_ref, bi, t, c)
                ypart = term if ypart is None else ypart + term
            y = _head_allreduce(ypart)
            for half in range(2):
                tiles = [y[2 * SUBLANES * g + SUBLANES * half:2 * SUBLANES * g + SUBLANES * (half + 1), :]
                         for g in range(n_groups)]
                y_ref[bi, 0, pl.ds(SUBLANES * half * tm + t, SUBLANES, stride=tm), :] = pick_groups(tiles)
        return carry

    lax.fori_loop(0, tm, step, 0, unroll=2)


def _rwkv_scan(r, w, k, na, kb, v, batch, seq):
    tm = RWKV_TILE
    rows = (RWKV_WIDTH // LANES) * tm
    shape4 = (batch, seq // tm, rows, LANES)
    spec = pl.BlockSpec((batch, 1, rows, LANES), lambda i: (0, i, 0, 0))
    y = pl.pallas_call(
        functools.partial(_rwkv_scan_kernel, tm=tm),
        grid=(seq // tm,),
        in_specs=[spec] * 6,
        out_specs=spec,
        out_shape=jax.ShapeDtypeStruct(shape4, F32),
        scratch_shapes=[pltpu.VMEM((batch, HEAD_DIM, RWKV_WIDTH), F32)],
        compiler_params=_params("arbitrary"),
        name="rwkv_scan",
    )(*[x.reshape(shape4) for x in (r, w, k, na, kb, v)])
    return y.reshape(batch * seq // tm, rows, LANES)


def _rwkv_post_kernel(y_ref, r_ref, k_ref, v_ref, g_ref, lnw_ref, lnb_ref, rk_ref, o_ref, *, tm):
    n_chunks = RWKV_WIDTH // LANES
    total = None
    bonus = None
    for c in range(n_chunks):
        rows = slice(c * tm, (c + 1) * tm)
        y = y_ref[0, rows, :]
        total = y if total is None else total + y
        term = r_ref[0, rows, :] * k_ref[0, rows, :] * rk_ref[:, c * LANES:(c + 1) * LANES]
        bonus = term if bonus is None else bonus + term
    mean = _head_allreduce(total) * (1.0 / HEAD_DIM)
    bonus = _head_allreduce(bonus)
    sq = None
    for c in range(n_chunks):
        d = y_ref[0, c * tm:(c + 1) * tm, :] - mean
        sq = d * d if sq is None else sq + d * d
    rstd = lax.rsqrt(_head_allreduce(sq) * (1.0 / HEAD_DIM) + RWKV_GN_EPS)
    for c in range(n_chunks):
        rows = slice(c * tm, (c + 1) * tm)
        cols = slice(c * LANES, (c + 1) * LANES)
        yn = (y_ref[0, rows, :] - mean) * rstd * lnw_ref[:, cols] + lnb_ref[:, cols]
        o_ref[:, cols] = ((yn + bonus * v_ref[0, rows, :]) * g_ref[0, rows, :]).astype(o_ref.dtype)


def _rwkv_post(y, r, k, v, g, ln_w, ln_b, r_k):
    tm = RWKV_TILE
    n_tiles, rows, _ = y.shape
    tile = pl.BlockSpec((1, rows, LANES), lambda i: (i, 0, 0))
    vec = pl.BlockSpec((1, RWKV_WIDTH), lambda i: (0, 0))
    return pl.pallas_call(
        functools.partial(_rwkv_post_kernel, tm=tm),
        grid=(n_tiles,),
        in_specs=[tile] * 5 + [vec] * 3,
        out_specs=pl.BlockSpec((tm, RWKV_WIDTH), lambda i: (i, 0)),
        out_shape=jax.ShapeDtypeStruct((n_tiles * tm, RWKV_WIDTH), BF16),
        compiler_params=_params("parallel"),
        name="rwkv_post",
    )(y, r, k, v, g, ln_w.reshape(1, -1), ln_b.reshape(1, -1), r_k.reshape(1, -1))


def _mlp_up_kernel(x_ref, wg_ref, wu_ref, cwg_ref, cwu_ref, cbg_ref, cbu_ref, o_ref,
                   work, carry_g, carry_u, *, tm, tiles_per_seq):
    i = pl.program_id(0)
    j = pl.program_id(1)
    seq_start = (i % tiles_per_seq) == 0
    x = x_ref[...]

    @pl.when((i == 0) & (j == 0))
    def _():
        carry_g[...] = jnp.zeros_like(carry_g)
        carry_u[...] = jnp.zeros_like(carry_u)

    def conv(w_ref, carry, cw_ref, cb_ref):
        u = jnp.dot(x, w_ref[...], preferred_element_type=F32)
        work[0:SUBLANES, :] = jnp.where(seq_start, 0.0, carry[j])
        work[SUBLANES:SUBLANES + tm, :] = u
        carry[j] = u[tm - SUBLANES:tm, :]
        lo = SUBLANES - (CONV_WIDTH - 1)
        out = work[lo:lo + tm, :] * cw_ref[0:1, :]
        out = out + work[lo + 1:lo + 1 + tm, :] * cw_ref[1:2, :]
        out = out + u * cw_ref[2:3, :]
        return out + cb_ref[...]

    gate = conv(wg_ref, carry_g, cwg_ref, cbg_ref)
    up = conv(wu_ref, carry_u, cwu_ref, cbu_ref)
    o_ref[...] = (gate * jax.nn.sigmoid(gate) * up).astype(o_ref.dtype)


def _mlp_up(x, w_gate, w_up, cw_gate, cw_up, cb_gate, cb_up, seq, *, tm=1024, tn=256):
    m, k = x.shape
    n = w_gate.shape[1]
    n_tiles = n // tn
    wspec = pl.BlockSpec((k, tn), lambda i, j: (0, j))
    cwspec = pl.BlockSpec((CONV_WIDTH, tn), lambda i, j: (0, j))
    cbspec = pl.BlockSpec((1, tn), lambda i, j: (0, j))
    return pl.pallas_call(
        functools.partial(_mlp_up_kernel, tm=tm, tiles_per_seq=seq // tm),
        grid=(m // tm, n_tiles),
        in_specs=[pl.BlockSpec((tm, k), lambda i, j: (i, 0)), wspec, wspec, cwspec, cwspec, cbspec, cbspec],
        out_specs=pl.BlockSpec((tm, tn), lambda i, j: (i, j)),
        out_shape=jax.ShapeDtypeStruct((m, n), BF16),
        scratch_shapes=[pltpu.VMEM((tm + SUBLANES, tn), F32),
                        pltpu.VMEM((n_tiles, SUBLANES, tn), F32),
                        pltpu.VMEM((n_tiles, SUBLANES, tn), F32)],
        compiler_params=_params("arbitrary", "arbitrary"),
        name="mlp_up_conv_gate",
    )(x, w_gate, w_up, cw_gate, cw_up, cb_gate.reshape(1, n), cb_up.reshape(1, n))


def _ple_embed_kernel(p_ref, w_ref, g_ref, o_ref):
    e = jnp.dot(p_ref[...].astype(BF16), w_ref[...], preferred_element_type=F32)
    ms = jnp.mean(e * e, axis=-1, keepdims=True)
    o_ref[...] = e * lax.rsqrt(ms + NORM_EPS) * g_ref[...]


def _ple_embed(p, w, g, *, tm=256):
    t, k = p.shape
    d = w.shape[1]
    return pl.pallas_call(
        _ple_embed_kernel,
        grid=(t // tm,),
        in_specs=[pl.BlockSpec((tm, k), lambda i: (i, 0)), pl.BlockSpec((k, d), lambda i: (0, 0)),
                  pl.BlockSpec((1, d), lambda i: (0, 0))],
        out_specs=pl.BlockSpec((tm, d), lambda i: (i, 0)),
        out_shape=jax.ShapeDtypeStruct((t, d), F32),
        compiler_params=_params("parallel"),
        name="ple_embed",
    )(p, w, g.reshape(1, d))


def _to_k_major(w):
    lead = w.shape[:-1]
    return w.reshape(*lead, N_RWKV_HEADS, HEAD_DIM).swapaxes(-1, -2).reshape(*lead, RWKV_WIDTH)


def _split_rwkv_cols(a, pad_value=0.0):
    pad = jnp.full(a.shape[:-1] + (GATE_LORA_PAD - GATE_LORA,), pad_value, a.dtype)
    blocks = [_to_k_major(a[..., i * RWKV_WIDTH:(i + 1) * RWKV_WIDTH]) for i in range(3)]
    return jnp.concatenate(blocks + [a[..., 3 * RWKV_WIDTH:], pad], axis=-1)


def kernel(x, p, positions, attn_norm_g, w_in, q_norm_g, k_norm_g, rwkv_mu, w0, w_decay_up, a0, w_iclr_up,
           w_gate_up, k_k, k_a, r_k, ln_x_w, ln_x_b, w_out, mlp_norm_g, w_mlp_up, conv_w, conv_b, w_mlp_down,
           w_ple_proj, ple_norm_g, w_ple_gate):
    batch, seq, d_model = x.shape
    assert w_in.shape[0] == 1 and seq % ATTN_WINDOW == 0
    t = batch * seq
    d_ff = w_mlp_down.shape[1]
    x2 = x.reshape(t, d_model)

    w_qkv = w_in[0][:, :3 * ATTN_WIDTH].astype(BF16)
    w_rwkv = _split_rwkv_cols(w_in[0][:, 3 * ATTN_WIDTH:]).astype(BF16)
    mu = _split_rwkv_cols(rwkv_mu[0])
    w_gate_lora = jnp.concatenate([_to_k_major(w_gate_up[0]),
                                   jnp.zeros((GATE_LORA_PAD - GATE_LORA, RWKV_WIDTH), F32)]).astype(BF16)
    w_out_rows = jnp.concatenate([
        w_out[0][:ATTN_WIDTH],
        w_out[0][ATTN_WIDTH:].reshape(N_RWKV_HEADS, HEAD_DIM, d_model).swapaxes(0, 1).reshape(RWKV_WIDTH, d_model),
    ]).astype(BF16)

    xn = _rmsnorm_bf16(x2, attn_norm_g[0])
    proj_qkv = _matmul(xn, w_qkv, tm=1024, tn=512, name="in_proj_attn")
    proj_rwkv = _matmul(xn, w_rwkv, tm=1024, tn=768, name="in_proj_rwkv")

    qk = _qk_prep(proj_qkv, positions, q_norm_g[0], k_norm_g[0])
    attn = _attention(qk, proj_qkv, batch, seq)

    r, w, k, v, na, kb, g = _rwkv_prep(
        proj_rwkv, seq, mu, _to_k_major(w0[0]), _to_k_major(a0[0]), _to_k_major(k_k[0]), _to_k_major(k_a[0]),
        _to_k_major(w_decay_up[0]).astype(BF16), _to_k_major(w_iclr_up[0]).astype(BF16), w_gate_lora)
    y = _rwkv_scan(r, w, k, na, kb, v, batch, seq)
    rwkv = _rwkv_post(y, r, k, v, g, _to_k_major(ln_x_w[0]), _to_k_major(ln_x_b[0]),
                      r_k[0].T.reshape(RWKV_WIDTH))

    h1 = _out_proj(attn, rwkv, w_out_rows, x2)

    hn = _rmsnorm_bf16(h1, mlp_norm_g[0])
    w_up = w_mlp_up[0]
    act = _mlp_up(hn, w_up[:, :d_ff].astype(BF16), w_up[:, d_ff:].astype(BF16),
                  conv_w[0][:, :d_ff], conv_w[0][:, d_ff:], conv_b[0][:d_ff], conv_b[0][d_ff:], seq)
    h2, h2_bf16 = _matmul(act, w_mlp_down[0].astype(BF16), tm=512, tn=256, residual=h1, emit_bf16=True,
                          name="mlp_down")

    e = _ple_embed(p[0].reshape(t, -1), w_ple_proj[0].astype(BF16), ple_norm_g[0])
    out = _matmul(h2_bf16, w_ple_gate[0].astype(BF16), tm=1024, tn=512, residual=h2, gate_e=e,
                  name="ple_gate")
    return out.reshape(batch, seq, d_model)
```

```python
import functools

import jax
import jax.numpy as jnp
from jax import lax
from jax.experimental import pallas as pl
from jax.experimental.pallas import tpu as pltpu

F32 = jnp.float32
BF16 = jnp.bfloat16

HEAD_DIM = 64
N_ATTN_HEADS = 32
N_RWKV_HEADS = 32
ATTN_WIDTH = N_ATTN_HEADS * HEAD_DIM
RWKV_WIDTH = N_RWKV_HEADS * HEAD_DIM
ATTN_BLOCK = 128
ATTN_WINDOW = 2048
ATTN_UNROLL = 8
Q_SCALE = HEAD_DIM ** -0.5 * 1.4426950408889634
ROPE_THETA = 10000.0
DECAY_LORA = 128
ICLR_LORA = 128
GATE_LORA = 480
GATE_LORA_PAD = 512
RWKV_TILE = 128
RWKV_PITCH = RWKV_TILE + 8
CONV_WIDTH = 3
MLP_ROW_SPLIT = 2
NORM_EPS = 1e-6
RWKV_GN_EPS = 64e-5
LANES = 128
SUBLANES = 8
VMEM_LIMIT = 56 * 1024 * 1024
MASK_VALUE = -1e30


def _params(*semantics):
    return pltpu.CompilerParams(dimension_semantics=semantics, vmem_limit_bytes=VMEM_LIMIT)


def _rmsnorm_kernel(x_ref, g_ref, o_ref):
    x = x_ref[...]
    ms = jnp.mean(x * x, axis=-1, keepdims=True)
    o_ref[...] = (x * lax.rsqrt(ms + NORM_EPS) * g_ref[...]).astype(o_ref.dtype)


def _rmsnorm_bf16(x, g, *, tm=256):
    m, d = x.shape
    return pl.pallas_call(
        _rmsnorm_kernel,
        grid=(m // tm,),
        in_specs=[pl.BlockSpec((tm, d), lambda i: (i, 0)), pl.BlockSpec((1, d), lambda i: (0, 0))],
        out_specs=pl.BlockSpec((tm, d), lambda i: (i, 0)),
        out_shape=jax.ShapeDtypeStruct((m, d), BF16),
        compiler_params=_params("parallel"),
        name="rmsnorm_bf16",
    )(x, g.reshape(1, d))


def _matmul_kernel(*refs, has_res, has_gate, emit_bf16):
    x_ref, w_ref = refs[0], refs[1]
    pos = 2
    res_ref = e_ref = None
    if has_res:
        res_ref = refs[pos]
        pos += 1
    if has_gate:
        e_ref = refs[pos]
        pos += 1
    o_ref = refs[pos]
    acc = jnp.dot(x_ref[...], w_ref[...], preferred_element_type=F32)
    if has_gate:
        acc = jax.nn.sigmoid(acc) * e_ref[...]
    if has_res:
        acc = res_ref[...] + acc
    o_ref[...] = acc
    if emit_bf16:
        refs[pos + 1][...] = acc.astype(BF16)


def _matmul(x, w, *, tm, tn, residual=None, gate_e=None, emit_bf16=False, name):
    m, k = x.shape
    n = w.shape[1]
    tile = pl.BlockSpec((tm, tn), lambda i, j: (i, j))
    in_specs = [pl.BlockSpec((tm, k), lambda i, j: (i, 0)), pl.BlockSpec((k, tn), lambda i, j: (0, j))]
    args = [x, w]
    if residual is not None:
        in_specs.append(tile)
        args.append(residual)
    if gate_e is not None:
        in_specs.append(tile)
        args.append(gate_e)
    out_specs = [tile]
    out_shape = [jax.ShapeDtypeStruct((m, n), F32)]
    if emit_bf16:
        out_specs.append(tile)
        out_shape.append(jax.ShapeDtypeStruct((m, n), BF16))
    outs = pl.pallas_call(
        functools.partial(_matmul_kernel, has_res=residual is not None, has_gate=gate_e is not None,
                          emit_bf16=emit_bf16),
        grid=(m // tm, n // tn),
        in_specs=in_specs,
        out_specs=out_specs,
        out_shape=out_shape,
        compiler_params=_params("parallel", "arbitrary"),
        name=name,
    )(*args)
    return outs if emit_bf16 else outs[0]


def _out_proj_kernel(a_ref, r_ref, wa_ref, wr_ref, res_ref, o_ref, a_bf16):
    @pl.when(pl.program_id(1) == 0)
    def _():
        a_bf16[...] = a_ref[...].astype(BF16)

    acc = jnp.dot(a_bf16[...], wa_ref[...], preferred_element_type=F32)
    acc = acc + jnp.dot(r_ref[...], wr_ref[...], preferred_element_type=F32)
    o_ref[...] = res_ref[...] + acc


def _out_proj(attn, rwkv, w, residual, *, tm=1024, tn=512):
    m, ka = attn.shape
    kr = rwkv.shape[1]
    n = w.shape[1]
    assert ka == kr
    tile = pl.BlockSpec((tm, tn), lambda i, j: (i, j))
    return pl.pallas_call(
        _out_proj_kernel,
        grid=(m // tm, n // tn),
        in_specs=[pl.BlockSpec((tm, ka), lambda i, j: (i, 0)), pl.BlockSpec((tm, kr), lambda i, j: (i, 0)),
                  pl.BlockSpec((ka, tn), lambda i, j: (0, j)), pl.BlockSpec((kr, tn), lambda i, j: (1, j)),
                  tile],
        out_specs=tile,
        out_shape=jax.ShapeDtypeStruct((m, n), F32),
        scratch_shapes=[pltpu.VMEM((tm, ka), BF16)],
        compiler_params=_params("parallel", "arbitrary"),
        name="out_proj",
    )(attn, rwkv, w, w, residual)


def _group_sum_lanes(x):
    rows = lax.broadcasted_iota(jnp.int32, (LANES, LANES), 0) // HEAD_DIM
    cols = lax.broadcasted_iota(jnp.int32, (LANES, LANES), 1) // HEAD_DIM
    ones = jnp.where(rows == cols, 1.0, 0.0).astype(BF16)
    hi = x.astype(BF16)
    lo = (x - hi.astype(F32)).astype(BF16)
    return (jnp.dot(hi, ones, preferred_element_type=F32) + jnp.dot(lo, ones, preferred_element_type=F32))


def _qk_prep_kernel(x_ref, pos_ref, invf_ref, g_ref, o_ref, cos_ref, sin_ref, *, n_q_blocks):
    j = pl.program_id(1)
    lane = lax.broadcasted_iota(jnp.int32, cos_ref.shape, 1)
    first_half = (lane % HEAD_DIM) < (HEAD_DIM // 2)

    @pl.when(j == 0)
    def _():
        ang = pos_ref[...].astype(F32) * invf_ref[...]
        cos_ref[...] = jnp.cos(ang)
        sin = jnp.sin(ang)
        sin_ref[...] = jnp.where(first_half, -sin, sin)

    scale = jnp.where(j < n_q_blocks, Q_SCALE, 1.0)
    for c in range(x_ref.shape[1] // LANES):
        cols = slice(c * LANES, (c + 1) * LANES)
        x = x_ref[:, cols]
        ms = _group_sum_lanes(x * x) * (1.0 / HEAD_DIM)
        y = x * lax.rsqrt(ms + NORM_EPS) * g_ref[0]
        partner = jnp.where(first_half, pltpu.roll(y, LANES - HEAD_DIM // 2, 1), pltpu.roll(y, HEAD_DIM // 2, 1))
        o_ref[:, cols] = (y * cos_ref[...] + partner * sin_ref[...]) * scale


def _qk_prep(proj_qkv, positions, q_g, k_g, *, tm=512, tn=512):
    t = proj_qkv.shape[0]
    half = HEAD_DIM // 2
    inv_freq = ROPE_THETA ** (-jnp.arange(half, dtype=F32) / half)
    invf = jnp.tile(inv_freq, LANES // half).reshape(1, LANES)
    gains = jnp.stack([jnp.tile(q_g, LANES // HEAD_DIM), jnp.tile(k_g, LANES // HEAD_DIM)]).reshape(2, 1, LANES)
    tn = min(tn, ATTN_WIDTH)
    n_q_blocks = ATTN_WIDTH // tn
    return pl.pallas_call(
        functools.partial(_qk_prep_kernel, n_q_blocks=n_q_blocks),
        grid=(t // tm, 2 * n_q_blocks),
        in_specs=[
            pl.BlockSpec((tm, tn), lambda i, j: (i, j)),
            pl.BlockSpec((tm, 1), lambda i, j: (i, 0)),
            pl.BlockSpec((1, LANES), lambda i, j: (0, 0)),
            pl.BlockSpec((1, 1, LANES), lambda i, j: (j // n_q_blocks, 0, 0)),
        ],
        out_specs=pl.BlockSpec((tm, tn), lambda i, j: (i, j)),
        out_shape=jax.ShapeDtypeStruct((t, 2 * ATTN_WIDTH), F32),
        scratch_shapes=[pltpu.VMEM((tm, LANES), F32), pltpu.VMEM((tm, LANES), F32)],
        compiler_params=_params("parallel", "arbitrary"),
        name="qk_norm_rope",
    )(proj_qkv, positions.reshape(t, 1), invf, gains)


def _attn_block(q, k, v, valid, old):
    lane_q = lax.broadcasted_iota(jnp.int32, q.shape, 1)
    lane_v = lax.broadcasted_iota(jnp.int32, v.shape, 1)
    kb = k.astype(BF16)
    outs = []
    for h in range(2):
        own_q = (lane_q < HEAD_DIM) if h == 0 else (lane_q >= HEAD_DIM)
        own_v = (lane_v < HEAD_DIM) if h == 0 else (lane_v >= HEAD_DIM)
        qh = jnp.where(own_q, q, 0.0).astype(BF16)
        s = lax.dot_general(qh, kb, (((1,), (1,)), ((), ())), preferred_element_type=F32)
        s = jnp.where(valid, s, MASK_VALUE)
        mb = jnp.max(s, axis=1, keepdims=True)
        vh = jnp.where(own_v, v, 1.0).astype(BF16)
        if old is None:
            m_new = jnp.broadcast_to(mb, q.shape)
            p = jnp.exp2(s - mb)
            acc = jnp.dot(p.astype(BF16), vh, preferred_element_type=F32)
        else:
            acc_old, m_old = old[h]
            m_new = jnp.maximum(m_old, mb)
            alpha = jnp.exp2(m_old - m_new)
            p = jnp.exp2(s - jnp.concatenate([m_new, m_new], axis=1))
            acc = acc_old * alpha + jnp.dot(p.astype(BF16), vh, preferred_element_type=F32)
        outs.append((acc, m_new))
    return outs


def _attn_kernel(q_ref, kp_ref, kc_ref, vp_ref, vc_ref, o_ref,
                 kn, vn, k4, v4, q4, acc_n, m_n, acc_d, m_d):
    blk = ATTN_BLOCK
    win = ATTN_WINDOW
    quarter = win // 4
    n_blocks = win // blk
    has_prev_window = pl.program_id(2) > 0

    kn[0:win] = kp_ref[0]
    kn[win:2 * win] = kc_ref[0]
    vn[0:win] = vp_ref[0]
    vn[win:2 * win] = vc_ref[0]
    for r in range(4):
        base = 2 * quarter * r
        k4[base:base + quarter] = kp_ref[0, pl.ds(r, quarter, stride=4), :]
        k4[base + quarter:base + 2 * quarter] = kc_ref[0, pl.ds(r, quarter, stride=4), :]
        v4[base:base + quarter] = vp_ref[0, pl.ds(r, quarter, stride=4), :]
        v4[base + quarter:base + 2 * quarter] = vc_ref[0, pl.ds(r, quarter, stride=4), :]
        q4[quarter * r:quarter * (r + 1)] = q_ref[0, pl.ds(r, quarter, stride=4), :]

    qi = lax.broadcasted_iota(jnp.int32, (blk, 2 * blk), 0)
    ki = lax.broadcasted_iota(jnp.int32, (blk, 2 * blk), 1)
    band = (ki >= qi) & (ki <= qi + blk)
    in_cur = ki >= blk

    def valid_mask(has_prev):
        return band & (in_cur | has_prev)

    def dilation1(n, carry):
        q = q_ref[0, pl.ds(pl.multiple_of(n * blk, blk), blk), :]
        start = pl.multiple_of(win + (n - 1) * blk, blk)
        k = kn[pl.ds(start, 2 * blk), :]
        v = vn[pl.ds(start, 2 * blk), :]
        outs = _attn_block(q, k, v, valid_mask(has_prev_window | (n > 0)), None)
        rows = pl.ds(pl.multiple_of(n * blk, blk), blk)
        for h in range(2):
            acc_n[h, rows, :] = outs[h][0]
            m_n[h, rows, :] = outs[h][1]
        return carry

    lax.fori_loop(0, n_blocks, dilation1, 0, unroll=ATTN_UNROLL)

    def dilation4(idx, carry):
        r = idx // 4
        j = idx % 4
        q = q4[pl.ds(pl.multiple_of(quarter * r + blk * j, blk), blk), :]
        start = pl.multiple_of(2 * quarter * r + quarter + blk * (j - 1), blk)
        k = k4[pl.ds(start, 2 * blk), :]
        v = v4[pl.ds(start, 2 * blk), :]
        nat = pl.ds(r + 4 * blk * j, blk, stride=4)
        old = [(acc_n[h, nat, :], m_n[h, nat, :]) for h in range(2)]
        outs = _attn_block(q, k, v, valid_mask(has_prev_window | (j > 0)), old)
        rows = pl.ds(pl.multiple_of(quarter * r + blk * j, blk), blk)
        for h in range(2):
            acc_d[h, rows, :] = outs[h][0]
            m_d[h, rows, :] = outs[h][1]
        return carry

    lax.fori_loop(0, n_blocks, dilation4, 0, unroll=ATTN_UNROLL)

    def dilation16(res, carry):
        r4 = res % 4
        c = res // 4
        rows = pl.ds(quarter * r4 + c, blk, stride=4)
        q = q4[rows, :]
        keys = pl.ds(2 * quarter * r4 + c, 2 * blk, stride=4)
        k = k4[keys, :]
        v = v4[keys, :]
        old = [(acc_d[h, rows, :], m_d[h, rows, :]) for h in range(2)]
        outs = _attn_block(q, k, v, valid_mask(has_prev_window), old)
        for h in range(2):
            acc_d[h, rows, :] = outs[h][0]
            m_d[h, rows, :] = outs[h][1]
        return carry

    lax.fori_loop(0, n_blocks, dilation16, 0, unroll=ATTN_UNROLL)

    lane = lax.broadcasted_iota(jnp.int32, (blk, LANES), 1)
    head0 = lane < HEAD_DIM

    def finalize(idx, carry):
        r = idx // 4
        j = idx % 4
        rows = pl.ds(pl.multiple_of(quarter * r + blk * j, blk), blk)
        a0 = acc_d[0, rows, :]
        a1 = acc_d[1, rows, :]
        num = jnp.where(head0, a0, a1)
        den = pltpu.roll(jnp.where(head0, a1, a0), HEAD_DIM, 1)
        o_ref[0, pl.ds(r + 4 * blk * j, blk, stride=4), :] = num / den
        return carry

    lax.fori_loop(0, n_blocks, finalize, 0, unroll=ATTN_UNROLL)


def _attention(qk, proj_qkv, batch, seq):
    win = ATTN_WINDOW
    n_pairs = ATTN_WIDTH // LANES
    qk3 = qk.reshape(batch, seq, 2 * ATTN_WIDTH)
    pv3 = proj_qkv.reshape(batch, seq, 3 * ATTN_WIDTH)
    blockspec = lambda fn: pl.BlockSpec((1, win, LANES), fn)
    prev = lambda w: jnp.maximum(w - 1, 0)
    scratch = [pltpu.VMEM((2 * win, LANES), F32)] * 4 + [pltpu.VMEM((win, LANES), F32)] + \
              [pltpu.VMEM((2, win, LANES), F32)] * 4
    out = pl.pallas_call(
        _attn_kernel,
        grid=(batch, n_pairs, seq // win),
        in_specs=[
            blockspec(lambda b, hp, w: (b, w, hp)),
            blockspec(lambda b, hp, w: (b, prev(w), n_pairs + hp)),
            blockspec(lambda b, hp, w: (b, w, n_pairs + hp)),
            blockspec(lambda b, hp, w: (b, prev(w), 2 * n_pairs + hp)),
            blockspec(lambda b, hp, w: (b, w, 2 * n_pairs + hp)),
        ],
        out_specs=blockspec(lambda b, hp, w: (b, w, hp)),
        out_shape=jax.ShapeDtypeStruct((batch, seq, ATTN_WIDTH), F32),
        scratch_shapes=scratch,
        compiler_params=_params("parallel", "parallel", "arbitrary"),
        name="dilated_attention",
    )(qk3, qk3, qk3, pv3, pv3)
    return out.reshape(batch * seq, ATTN_WIDTH)


def _head_allreduce(x):
    return x + pltpu.roll(x, 32, 1) + pltpu.roll(x, 64, 1) + pltpu.roll(x, 96, 1)


def _softplus(x):
    return jnp.maximum(x, 0.0) + jnp.log1p(jnp.exp(-jnp.abs(x)))


def _rwkv_prep_kernel(p_ref, halo_ref, mu_ref, w0_ref, a0_ref, kk_ref, ka_ref, wd_ref, wa_ref, wg_ref,
                      r_out, w_out, k_out, v_out, na_out, kb_out, g_out, shift, kmix, *, tm, tiles_per_seq):
    seq_start = (pl.program_id(0) % tiles_per_seq) == 0
    n_chunks = RWKV_WIDTH // LANES

    def mixed(c0):
        cols = slice(c0, c0 + LANES)
        x = p_ref[:, cols]
        shift[SUBLANES - 1:SUBLANES, :] = jnp.where(seq_start, 0.0, halo_ref[SUBLANES - 1:SUBLANES, cols])
        shift[SUBLANES:SUBLANES + tm, :] = x
        prev = shift[SUBLANES - 1:SUBLANES - 1 + tm, :]
        return x + (prev - x) * mu_ref[:, cols]

    lora0 = 3 * RWKV_WIDTH
    th = jnp.tanh(mixed(lora0)).astype(BF16)
    ad = mixed(lora0 + DECAY_LORA).astype(BF16)
    gate0 = lora0 + DECAY_LORA + ICLR_LORA
    sg = jnp.concatenate([jax.nn.sigmoid(mixed(gate0 + c * LANES)) for c in range(GATE_LORA_PAD // LANES)],
                         axis=1).astype(BF16)

    ssq = jnp.zeros((tm, LANES), F32)
    for c in range(n_chunks):
        cols = slice(c * LANES, (c + 1) * LANES)
        km = mixed(RWKV_WIDTH + c * LANES)
        kmix[:, cols] = km
        kk = km * kk_ref[:, cols]
        ssq = ssq + kk * kk
    inv_norm = 1.0 / jnp.maximum(jnp.sqrt(_head_allreduce(ssq)), 1e-12)

    for c in range(n_chunks):
        cols = slice(c * LANES, (c + 1) * LANES)
        rows = slice(c * RWKV_PITCH, c * RWKV_PITCH + tm)
        pad = slice(c * RWKV_PITCH + tm, (c + 1) * RWKV_PITCH)
        for out in (r_out, w_out, k_out, v_out, na_out, kb_out, g_out):
            out[0, pad, :] = jnp.zeros((RWKV_PITCH - tm, LANES), F32)
        r = mixed(c * LANES)
        v = mixed(2 * RWKV_WIDTH + c * LANES)
        k = kmix[:, cols]
        z = w0_ref[:, cols] + jnp.dot(th, wd_ref[:, cols], preferred_element_type=F32)
        w_raw = -_softplus(-z) - 0.5
        a = jax.nn.sigmoid(a0_ref[:, cols] + jnp.dot(ad, wa_ref[:, cols], preferred_element_type=F32))
        kkn = k * kk_ref[:, cols] * inv_norm
        r_out[0, rows, :] = r
        w_out[0, rows, :] = jnp.exp(-jnp.exp(w_raw))
        k_out[0, rows, :] = k * (1.0 + (a - 1.0) * ka_ref[:, cols])
        v_out[0, rows, :] = v
        na_out[0, rows, :] = -kkn
        kb_out[0, rows, :] = kkn * a
        g_out[0, rows, :] = jnp.dot(sg, wg_ref[:, cols], preferred_element_type=F32)


def _rwkv_prep(proj, seq, mu, w0, a0, k_k, k_a, w_decay, w_iclr, w_gate):
    t, width = proj.shape
    tm = RWKV_TILE
    n_chunks = RWKV_WIDTH // LANES
    row = lambda v: v.reshape(1, -1)
    full = lambda a: pl.BlockSpec(a.shape, lambda i: (0, 0))
    halo_blocks = tm // SUBLANES
    params = [row(mu), row(w0), row(a0), row(k_k), row(k_a), w_decay, w_iclr, w_gate]
    out_spec = pl.BlockSpec((1, n_chunks * RWKV_PITCH, LANES), lambda i: (i, 0, 0))
    return pl.pallas_call(
        functools.partial(_rwkv_prep_kernel, tm=tm, tiles_per_seq=seq // tm),
        grid=(t // tm,),
        in_specs=[pl.BlockSpec((tm, width), lambda i: (i, 0)),
                  pl.BlockSpec((SUBLANES, width), lambda i: (jnp.maximum(i * halo_blocks - 1, 0), 0))]
                 + [full(a) for a in params],
        out_specs=[out_spec] * 7,
        out_shape=[jax.ShapeDtypeStruct((t // tm, n_chunks * RWKV_PITCH, LANES), F32)] * 7,
        scratch_shapes=[pltpu.VMEM((tm + SUBLANES, LANES), F32), pltpu.VMEM((tm, RWKV_WIDTH), F32)],
        compiler_params=_params("parallel"),
        name="rwkv_prep",
    )(proj, proj, *params)


def _rwkv_scan_kernel(r_ref, w_ref, k_ref, a_ref, b_ref, v_ref, y_ref, s_ref, *, tm):
    n_chunks = RWKV_WIDTH // LANES
    n_groups = LANES // N_RWKV_HEADS
    lane_group = lax.broadcasted_iota(jnp.int32, (SUBLANES, LANES), 1) // N_RWKV_HEADS

    @pl.when(pl.program_id(0) == 0)
    def _():
        s_ref[...] = jnp.zeros_like(s_ref)

    def row(ref, bi, t, c):
        return ref[bi, 0, pl.ds(c * RWKV_PITCH + t, 1), :]

    def pick_groups(tiles):
        out = tiles[n_groups - 1]
        for g in range(n_groups - 2, -1, -1):
            out = jnp.where(lane_group == g, tiles[g], out)
        return out

    def step(t, carry):
        for bi in range(s_ref.shape[0]):
            v_rows = {}
            for half in range(2):
                x = v_ref[bi, 0, pl.ds(SUBLANES * half * RWKV_PITCH + t, SUBLANES, stride=RWKV_PITCH), :]
                rolled = [x] + [pltpu.roll(x, N_RWKV_HEADS * j, 1) for j in range(1, n_groups)]
                for q in range(n_groups):
                    v_rows[(q, half)] = pick_groups([rolled[(g - q) % n_groups] for g in range(n_groups)])
            vt = jnp.concatenate([v_rows[(q, half)] for q in range(n_groups) for half in range(2)], axis=0)
            part = None
            for c in range(n_chunks):
                cols = slice(c * LANES, (c + 1) * LANES)
                term = s_ref[bi, :, cols] * row(a_ref, bi, t, c)
                part = term if part is None else part + term
            sa = _head_allreduce(part)
            ypart = None
            for c in range(n_chunks):
                cols = slice(c * LANES, (c + 1) * LANES)
                s_new = (s_ref[bi, :, cols] * row(w_ref, bi, t, c) + sa * row(b_ref, bi, t, c)
                         + vt * row(k_ref, bi, t, c))
                s_ref[bi, :, cols] = s_new
                term = s_new * row(r_ref, bi, t, c)
                ypart = term if ypart is None else ypart + term
            y = _head_allreduce(ypart)
            for half in range(2):
                tiles = [y[2 * SUBLANES * g + SUBLANES * half:2 * SUBLANES * g + SUBLANES * (half + 1), :]
                         for g in range(n_groups)]
                rows = pl.ds(SUBLANES * half * RWKV_PITCH + t, SUBLANES, stride=RWKV_PITCH)
                y_ref[bi, 0, rows, :] = pick_groups(tiles)
        return carry

    for bi in range(s_ref.shape[0]):
        for c in range(n_chunks):
            y_ref[bi, 0, c * RWKV_PITCH + tm:(c + 1) * RWKV_PITCH, :] = jnp.zeros((RWKV_PITCH - tm, LANES), F32)
    lax.fori_loop(0, tm, step, 0, unroll=2)


def _rwkv_scan(r, w, k, na, kb, v, batch, seq):
    tm = RWKV_TILE
    rows = (RWKV_WIDTH // LANES) * RWKV_PITCH
    shape4 = (batch, seq // tm, rows, LANES)
    spec = pl.BlockSpec((batch, 1, rows, LANES), lambda i: (0, i, 0, 0))
    y = pl.pallas_call(
        functools.partial(_rwkv_scan_kernel, tm=tm),
        grid=(seq // tm,),
        in_specs=[spec] * 6,
        out_specs=spec,
        out_shape=jax.ShapeDtypeStruct(shape4, F32),
        scratch_shapes=[pltpu.VMEM((batch, HEAD_DIM, RWKV_WIDTH), F32)],
        compiler_params=_params("arbitrary"),
        name="rwkv_scan",
    )(*[x.reshape(shape4) for x in (r, w, k, na, kb, v)])
    return y.reshape(batch * seq // tm, rows, LANES)


def _rwkv_post_kernel(y_ref, r_ref, k_ref, v_ref, g_ref, lnw_ref, lnb_ref, rk_ref, o_ref, *, tm):
    n_chunks = RWKV_WIDTH // LANES
    total = None
    bonus = None
    for c in range(n_chunks):
        rows = slice(c * RWKV_PITCH, c * RWKV_PITCH + tm)
        y = y_ref[0, rows, :]
        total = y if total is None else total + y
        term = r_ref[0, rows, :] * k_ref[0, rows, :] * rk_ref[:, c * LANES:(c + 1) * LANES]
        bonus = term if bonus is None else bonus + term
    mean = _head_allreduce(total) * (1.0 / HEAD_DIM)
    bonus = _head_allreduce(bonus)
    sq = None
    for c in range(n_chunks):
        d = y_ref[0, c * RWKV_PITCH:c * RWKV_PITCH + tm, :] - mean
        sq = d * d if sq is None else sq + d * d
    rstd = lax.rsqrt(_head_allreduce(sq) * (1.0 / HEAD_DIM) + RWKV_GN_EPS)
    for c in range(n_chunks):
        rows = slice(c * RWKV_PITCH, c * RWKV_PITCH + tm)
        cols = slice(c * LANES, (c + 1) * LANES)
        yn = (y_ref[0, rows, :] - mean) * rstd * lnw_ref[:, cols] + lnb_ref[:, cols]
        o_ref[:, cols] = ((yn + bonus * v_ref[0, rows, :]) * g_ref[0, rows, :]).astype(o_ref.dtype)


def _rwkv_post(y, r, k, v, g, ln_w, ln_b, r_k):
    tm = RWKV_TILE
    n_tiles, rows, _ = y.shape
    tile = pl.BlockSpec((1, rows, LANES), lambda i: (i, 0, 0))
    vec = pl.BlockSpec((1, RWKV_WIDTH), lambda i: (0, 0))
    return pl.pallas_call(
        functools.partial(_rwkv_post_kernel, tm=tm),
        grid=(n_tiles,),
        in_specs=[tile] * 5 + [vec] * 3,
        out_specs=pl.BlockSpec((tm, RWKV_WIDTH), lambda i: (i, 0)),
        out_shape=jax.ShapeDtypeStruct((n_tiles * tm, RWKV_WIDTH), BF16),
        compiler_params=_params("parallel"),
        name="rwkv_post",
    )(y, r, k, v, g, ln_w.reshape(1, -1), ln_b.reshape(1, -1), r_k.reshape(1, -1))


def _mlp_up_kernel(x_ref, wg_ref, wu_ref, cwg_ref, cwu_ref, cbg_ref, cbu_ref, o_ref,
                   work_g, work_u, carry_g, carry_u, *, tm, tiles_per_seq):
    seq_start = (pl.program_id(0) % tiles_per_seq) == 0
    j = pl.program_id(1)
    lo = SUBLANES - (CONV_WIDTH - 1)
    part = tm // MLP_ROW_SPLIT
    work_g[0:SUBLANES, :] = jnp.where(seq_start, 0.0, carry_g[j])
    work_u[0:SUBLANES, :] = jnp.where(seq_start, 0.0, carry_u[j])

    def conv(x, r0, w_ref, work, cw_ref, cb_ref):
        u = jnp.dot(x, w_ref[...], preferred_element_type=F32)
        work[SUBLANES + r0:SUBLANES + r0 + part, :] = u
        out = work[lo + r0:lo + r0 + part, :] * cw_ref[0:1, :]
        out = out + work[lo + 1 + r0:lo + 1 + r0 + part, :] * cw_ref[1:2, :]
        out = out + u * cw_ref[2:3, :]
        return out + cb_ref[...]

    for p in range(MLP_ROW_SPLIT):
        r0 = p * part
        x = x_ref[r0:r0 + part, :]
        gate = conv(x, r0, wg_ref, work_g, cwg_ref, cbg_ref)
        up = conv(x, r0, wu_ref, work_u, cwu_ref, cbu_ref)
        o_ref[r0:r0 + part, :] = (gate * jax.nn.sigmoid(gate) * up).astype(o_ref.dtype)
    carry_g[j] = work_g[tm:tm + SUBLANES, :]
    carry_u[j] = work_u[tm:tm + SUBLANES, :]


def _mlp_up(x, w_gate, w_up, cw_gate, cw_up, cb_gate, cb_up, seq, *, tm=1024, tn=256):
    m, k = x.shape
    n = w_gate.shape[1]
    n_tiles = n // tn
    wspec = pl.BlockSpec((k, tn), lambda i, j: (0, j))
    cwspec = pl.BlockSpec((CONV_WIDTH, tn), lambda i, j: (0, j))
    cbspec = pl.BlockSpec((1, tn), lambda i, j: (0, j))
    return pl.pallas_call(
        functools.partial(_mlp_up_kernel, tm=tm, tiles_per_seq=seq // tm),
        grid=(m // tm, n_tiles),
        in_specs=[pl.BlockSpec((tm, k), lambda i, j: (i, 0)), wspec, wspec, cwspec, cwspec, cbspec, cbspec],
        out_specs=pl.BlockSpec((tm, tn), lambda i, j: (i, j)),
        out_shape=jax.ShapeDtypeStruct((m, n), BF16),
        scratch_shapes=[pltpu.VMEM((tm + SUBLANES, tn), F32),
                        pltpu.VMEM((tm + SUBLANES, tn), F32),
                        pltpu.VMEM((n_tiles, SUBLANES, tn), F32),
                        pltpu.VMEM((n_tiles, SUBLANES, tn), F32)],
        compiler_params=_params("arbitrary", "arbitrary"),
        name="mlp_up_conv_gate",
    )(x, w_gate, w_up, cw_gate, cw_up, cb_gate.reshape(1, n), cb_up.reshape(1, n))


def _ple_embed_kernel(p_ref, w_ref, g_ref, o_ref):
    e = jnp.dot(p_ref[...].astype(BF16), w_ref[...], preferred_element_type=F32)
    ms = jnp.mean(e * e, axis=-1, keepdims=True)
    o_ref[...] = e * lax.rsqrt(ms + NORM_EPS) * g_ref[...]


def _ple_embed(p, w, g, *, tm=256):
    t, k = p.shape
    d = w.shape[1]
    return pl.pallas_call(
        _ple_embed_kernel,
        grid=(t // tm,),
        in_specs=[pl.BlockSpec((tm, k), lambda i: (i, 0)), pl.BlockSpec((k, d), lambda i: (0, 0)),
                  pl.BlockSpec((1, d), lambda i: (0, 0))],
        out_specs=pl.BlockSpec((tm, d), lambda i: (i, 0)),
        out_shape=jax.ShapeDtypeStruct((t, d), F32),
        compiler_params=_params("parallel"),
        name="ple_embed",
    )(p, w, g.reshape(1, d))


def _to_k_major(w):
    lead = w.shape[:-1]
    return w.reshape(*lead, N_RWKV_HEADS, HEAD_DIM).swapaxes(-1, -2).reshape(*lead, RWKV_WIDTH)


def _split_rwkv_cols(a, pad_value=0.0):
    pad = jnp.full(a.shape[:-1] + (GATE_LORA_PAD - GATE_LORA,), pad_value, a.dtype)
    blocks = [_to_k_major(a[..., i * RWKV_WIDTH:(i + 1) * RWKV_WIDTH]) for i in range(3)]
    return jnp.concatenate(blocks + [a[..., 3 * RWKV_WIDTH:], pad], axis=-1)


def kernel(x, p, positions, attn_norm_g, w_in, q_norm_g, k_norm_g, rwkv_mu, w0, w_decay_up, a0, w_iclr_up,
           w_gate_up, k_k, k_a, r_k, ln_x_w, ln_x_b, w_out, mlp_norm_g, w_mlp_up, conv_w, conv_b, w_mlp_down,
           w_ple_proj, ple_norm_g, w_ple_gate):
    batch, seq, d_model = x.shape
    assert w_in.shape[0] == 1 and seq % ATTN_WINDOW == 0
    t = batch * seq
    d_ff = w_mlp_down.shape[1]
    x2 = x.reshape(t, d_model)

    w_qkv = w_in[0][:, :3 * ATTN_WIDTH].astype(BF16)
    w_rwkv = _split_rwkv_cols(w_in[0][:, 3 * ATTN_WIDTH:]).astype(BF16)
    mu = _split_rwkv_cols(rwkv_mu[0])
    w_gate_lora = jnp.concatenate([_to_k_major(w_gate_up[0]),
                                   jnp.zeros((GATE_LORA_PAD - GATE_LORA, RWKV_WIDTH), F32)]).astype(BF16)
    w_out_rows = jnp.concatenate([
        w_out[0][:ATTN_WIDTH],
        w_out[0][ATTN_WIDTH:].reshape(N_RWKV_HEADS, HEAD_DIM, d_model).swapaxes(0, 1).reshape(RWKV_WIDTH, d_model),
    ]).astype(BF16)

    xn = _rmsnorm_bf16(x2, attn_norm_g[0])
    proj_qkv = _matmul(xn, w_qkv, tm=1024, tn=512, name="in_proj_attn")
    proj_rwkv = _matmul(xn, w_rwkv, tm=1024, tn=768, name="in_proj_rwkv")

    qk = _qk_prep(proj_qkv, positions, q_norm_g[0], k_norm_g[0])
    attn = _attention(qk, proj_qkv, batch, seq)

    r, w, k, v, na, kb, g = _rwkv_prep(
        proj_rwkv, seq, mu, _to_k_major(w0[0]), _to_k_major(a0[0]), _to_k_major(k_k[0]), _to_k_major(k_a[0]),
        _to_k_major(w_decay_up[0]).astype(BF16), _to_k_major(w_iclr_up[0]).astype(BF16), w_gate_lora)
    y = _rwkv_scan(r, w, k, na, kb, v, batch, seq)
    rwkv = _rwkv_post(y, r, k, v, g, _to_k_major(ln_x_w[0]), _to_k_major(ln_x_b[0]),
                      r_k[0].T.reshape(RWKV_WIDTH))

    h1 = _out_proj(attn, rwkv, w_out_rows, x2)

    hn = _rmsnorm_bf16(h1, mlp_norm_g[0])
    w_up = w_mlp_up[0]
    act = _mlp_up(hn, w_up[:, :d_ff].astype(BF16), w_up[:, d_ff:].astype(BF16),
                  conv_w[0][:, :d_ff], conv_w[0][:, d_ff:], conv_b[0][:d_ff], conv_b[0][d_ff:], seq)
    h2, h2_bf16 = _matmul(act, w_mlp_down[0].astype(BF16), tm=512, tn=512, residual=h1, emit_bf16=True,
                          name="mlp_down")

    e = _ple_embed(p[0].reshape(t, -1), w_ple_proj[0].astype(BF16), ple_norm_g[0])
    out = _matmul(h2_bf16, w_ple_gate[0].astype(BF16), tm=1024, tn=512, residual=h2, gate_e=e,
                  name="ple_gate")
    return out.reshape(batch, seq, d_model)
```

```python
import functools

import jax
import jax.numpy as jnp
from jax import lax
from jax.experimental import pallas as pl
from jax.experimental.pallas import tpu as pltpu

F32 = jnp.float32
BF16 = jnp.bfloat16

HEAD_DIM = 64
N_ATTN_HEADS = 32
N_RWKV_HEADS = 32
ATTN_WIDTH = N_ATTN_HEADS * HEAD_DIM
RWKV_WIDTH = N_RWKV_HEADS * HEAD_DIM
ATTN_BLOCK = 128
ATTN_WINDOW = 2048
ATTN_UNROLL = 8
Q_SCALE = HEAD_DIM ** -0.5 * 1.4426950408889634
ROPE_THETA = 10000.0
DECAY_LORA = 128
ICLR_LORA = 128
GATE_LORA = 480
GATE_LORA_PAD = 512
RWKV_TILE = 128
RWKV_PITCH = RWKV_TILE + 8
CONV_WIDTH = 3
MLP_ROW_SPLIT = 2
NORM_EPS = 1e-6
RWKV_GN_EPS = 64e-5
LANES = 128
SUBLANES = 8
VMEM_LIMIT = 56 * 1024 * 1024
MASK_VALUE = -1e30


def _params(*semantics):
    return pltpu.CompilerParams(dimension_semantics=semantics, vmem_limit_bytes=VMEM_LIMIT)


def _rmsnorm_kernel(x_ref, g_ref, o_ref):
    x = x_ref[...]
    ms = jnp.mean(x * x, axis=-1, keepdims=True)
    o_ref[...] = (x * lax.rsqrt(ms + NORM_EPS) * g_ref[...]).astype(o_ref.dtype)


def _rmsnorm_bf16(x, g, *, tm=256):
    m, d = x.shape
    return pl.pallas_call(
        _rmsnorm_kernel,
        grid=(m // tm,),
        in_specs=[pl.BlockSpec((tm, d), lambda i: (i, 0)), pl.BlockSpec((1, d), lambda i: (0, 0))],
        out_specs=pl.BlockSpec((tm, d), lambda i: (i, 0)),
        out_shape=jax.ShapeDtypeStruct((m, d), BF16),
        compiler_params=_params("parallel"),
        name="rmsnorm_bf16",
    )(x, g.reshape(1, d))


def _mxu_dot(x, w):
    return lax.dot_general(x, w, (((1,), (0,)), ((), ())), preferred_element_type=F32)


def _matmul_kernel(*refs, has_res, has_gate, emit_bf16):
    x_ref, w_ref = refs[0], refs[1]
    pos = 2
    res_ref = e_ref = None
    if has_res:
        res_ref = refs[pos]
        pos += 1
    if has_gate:
        e_ref = refs[pos]
        pos += 1
    o_ref = refs[pos]
    acc = _mxu_dot(x_ref[...], w_ref[...])
    if has_gate:
        acc = jax.nn.sigmoid(acc) * e_ref[...]
    if has_res:
        acc = res_ref[...] + acc
    o_ref[...] = acc
    if emit_bf16:
        refs[pos + 1][...] = acc.astype(BF16)


def _matmul(x, w, *, tm, tn, n=None, residual=None, gate_e=None, emit_bf16=False, name):
    m, k = x.shape
    n = w.shape[1] if n is None else n
    tile = pl.BlockSpec((tm, tn), lambda i, j: (i, j))
    in_specs = [pl.BlockSpec((tm, k), lambda i, j: (i, 0)), pl.BlockSpec((k, tn), lambda i, j: (0, j))]
    args = [x, w]
    if residual is not None:
        in_specs.append(tile)
        args.append(residual)
    if gate_e is not None:
        in_specs.append(tile)
        args.append(gate_e)
    out_specs = [tile]
    out_shape = [jax.ShapeDtypeStruct((m, n), F32)]
    if emit_bf16:
        out_specs.append(tile)
        out_shape.append(jax.ShapeDtypeStruct((m, n), BF16))
    outs = pl.pallas_call(
        functools.partial(_matmul_kernel, has_res=residual is not None, has_gate=gate_e is not None,
                          emit_bf16=emit_bf16),
        grid=(m // tm, n // tn),
        in_specs=in_specs,
        out_specs=out_specs,
        out_shape=out_shape,
        compiler_params=_params("parallel", "arbitrary"),
        name=name,
    )(*args)
    return outs if emit_bf16 else outs[0]


def _out_proj_kernel(a_ref, r_ref, wa_ref, wr_ref, res_ref, o_ref):
    acc = _mxu_dot(a_ref[...], wa_ref[...])
    acc = acc + _mxu_dot(r_ref[...], wr_ref[...])
    o_ref[...] = res_ref[...] + acc


def _out_proj(attn, rwkv, w_attn, w_rwkv, residual, *, tm=1024, tn=512):
    m, ka = attn.shape
    kr = rwkv.shape[1]
    n = w_attn.shape[1]
    tile = pl.BlockSpec((tm, tn), lambda i, j: (i, j))
    return pl.pallas_call(
        _out_proj_kernel,
        grid=(m // tm, n // tn),
        in_specs=[pl.BlockSpec((tm, ka), lambda i, j: (i, 0)), pl.BlockSpec((tm, kr), lambda i, j: (i, 0)),
                  pl.BlockSpec((ka, tn), lambda i, j: (0, j)), pl.BlockSpec((kr, tn), lambda i, j: (0, j)),
                  tile],
        out_specs=tile,
        out_shape=jax.ShapeDtypeStruct((m, n), F32),
        compiler_params=_params("parallel", "arbitrary"),
        name="out_proj",
    )(attn, rwkv, w_attn, w_rwkv, residual)


def _group_sum_lanes(x):
    rows = lax.broadcasted_iota(jnp.int32, (LANES, LANES), 0) // HEAD_DIM
    cols = lax.broadcasted_iota(jnp.int32, (LANES, LANES), 1) // HEAD_DIM
    ones = jnp.where(rows == cols, 1.0, 0.0).astype(BF16)
    hi = x.astype(BF16)
    lo = (x - hi.astype(F32)).astype(BF16)
    return (jnp.dot(hi, ones, preferred_element_type=F32) + jnp.dot(lo, ones, preferred_element_type=F32))


def _qk_prep_kernel(x_ref, pos_ref, invf_ref, g_ref, o_ref, cos_ref, sin_ref, *, n_q_blocks):
    j = pl.program_id(1)
    lane = lax.broadcasted_iota(jnp.int32, cos_ref.shape, 1)
    first_half = (lane % HEAD_DIM) < (HEAD_DIM // 2)

    @pl.when(j == 0)
    def _():
        ang = pos_ref[...].astype(F32) * invf_ref[...]
        cos_ref[...] = jnp.cos(ang)
        sin = jnp.sin(ang)
        sin_ref[...] = jnp.where(first_half, -sin, sin)

    scale = jnp.where(j < n_q_blocks, Q_SCALE, 1.0)
    for c in range(x_ref.shape[1] // LANES):
        cols = slice(c * LANES, (c + 1) * LANES)
        x = x_ref[:, cols]
        ms = _group_sum_lanes(x * x) * (1.0 / HEAD_DIM)
        y = x * lax.rsqrt(ms + NORM_EPS) * g_ref[0]
        partner = jnp.where(first_half, pltpu.roll(y, LANES - HEAD_DIM // 2, 1), pltpu.roll(y, HEAD_DIM // 2, 1))
        o_ref[:, cols] = (y * cos_ref[...] + partner * sin_ref[...]) * scale


def _qk_prep(proj_qkv, positions, q_g, k_g, *, tm=1024, tn=512):
    t = proj_qkv.shape[0]
    half = HEAD_DIM // 2
    inv_freq = ROPE_THETA ** (-jnp.arange(half, dtype=F32) / half)
    invf = jnp.tile(inv_freq, LANES // half).reshape(1, LANES)
    gains = jnp.stack([jnp.tile(q_g, LANES // HEAD_DIM), jnp.tile(k_g, LANES // HEAD_DIM)]).reshape(2, 1, LANES)
    tn = min(tn, ATTN_WIDTH)
    n_q_blocks = ATTN_WIDTH // tn
    return pl.pallas_call(
        functools.partial(_qk_prep_kernel, n_q_blocks=n_q_blocks),
        grid=(t // tm, 2 * n_q_blocks),
        in_specs=[
            pl.BlockSpec((tm, tn), lambda i, j: (i, j)),
            pl.BlockSpec((tm, 1), lambda i, j: (i, 0)),
            pl.BlockSpec((1, LANES), lambda i, j: (0, 0)),
            pl.BlockSpec((1, 1, LANES), lambda i, j: (j // n_q_blocks, 0, 0)),
        ],
        out_specs=pl.BlockSpec((tm, tn), lambda i, j: (i, j)),
        out_shape=jax.ShapeDtypeStruct((t, 2 * ATTN_WIDTH), F32),
        scratch_shapes=[pltpu.VMEM((tm, LANES), F32), pltpu.VMEM((tm, LANES), F32)],
        compiler_params=_params("parallel", "arbitrary"),
        name="qk_norm_rope",
    )(proj_qkv, positions.reshape(t, 1), invf, gains)


def _attn_block(q, k, v, valid, old):
    lane_q = lax.broadcasted_iota(jnp.int32, q.shape, 1)
    lane_v = lax.broadcasted_iota(jnp.int32, v.shape, 1)
    kb = k.astype(BF16)
    outs = []
    for h in range(2):
        own_q = (lane_q < HEAD_DIM) if h == 0 else (lane_q >= HEAD_DIM)
        own_v = (lane_v < HEAD_DIM) if h == 0 else (lane_v >= HEAD_DIM)
        qh = jnp.where(own_q, q, 0.0).astype(BF16)
        s = lax.dot_general(qh, kb, (((1,), (1,)), ((), ())), preferred_element_type=F32)
        s = jnp.where(valid, s, MASK_VALUE)
        mb = jnp.max(s, axis=1, keepdims=True)
        vh = jnp.where(own_v, v, 1.0).astype(BF16)
        if old is None:
            m_new = jnp.broadcast_to(mb, q.shape)
            p = jnp.exp2(s - mb)
            acc = jnp.dot(p.astype(BF16), vh, preferred_element_type=F32)
        else:
            acc_old, m_old = old[h]
            m_new = jnp.maximum(m_old, mb)
            alpha = jnp.exp2(m_old - m_new)
            p = jnp.exp2(s - jnp.concatenate([m_new, m_new], axis=1))
            acc = acc_old * alpha + jnp.dot(p.astype(BF16), vh, preferred_element_type=F32)
        outs.append((acc, m_new))
    return outs


def _attn_kernel(q_ref, kp_ref, kc_ref, vp_ref, vc_ref, o_ref,
                 kn, vn, k4, v4, q4, acc_n, m_n, acc_d, m_d):
    blk = ATTN_BLOCK
    win = ATTN_WINDOW
    quarter = win // 4
    n_blocks = win // blk
    has_prev_window = pl.program_id(2) > 0

    kn[0:win] = kp_ref[0]
    kn[win:2 * win] = kc_ref[0]
    vn[0:win] = vp_ref[0]
    vn[win:2 * win] = vc_ref[0]
    for r in range(4):
        base = 2 * quarter * r
        k4[base:base + quarter] = kp_ref[0, pl.ds(r, quarter, stride=4), :]
        k4[base + quarter:base + 2 * quarter] = kc_ref[0, pl.ds(r, quarter, stride=4), :]
        v4[base:base + quarter] = vp_ref[0, pl.ds(r, quarter, stride=4), :]
        v4[base + quarter:base + 2 * quarter] = vc_ref[0, pl.ds(r, quarter, stride=4), :]
        q4[quarter * r:quarter * (r + 1)] = q_ref[0, pl.ds(r, quarter, stride=4), :]

    qi = lax.broadcasted_iota(jnp.int32, (blk, 2 * blk), 0)
    ki = lax.broadcasted_iota(jnp.int32, (blk, 2 * blk), 1)
    band = (ki >= qi) & (ki <= qi + blk)
    in_cur = ki >= blk

    def valid_mask(has_prev):
        return band & (in_cur | has_prev)

    def dilation1(n, carry):
        q = q_ref[0, pl.ds(pl.multiple_of(n * blk, blk), blk), :]
        start = pl.multiple_of(win + (n - 1) * blk, blk)
        k = kn[pl.ds(start, 2 * blk), :]
        v = vn[pl.ds(start, 2 * blk), :]
        outs = _attn_block(q, k, v, valid_mask(has_prev_window | (n > 0)), None)
        rows = pl.ds(pl.multiple_of(n * blk, blk), blk)
        for h in range(2):
            acc_n[h, rows, :] = outs[h][0]
            m_n[h, rows, :] = outs[h][1]
        return carry

    lax.fori_loop(0, n_blocks, dilation1, 0, unroll=ATTN_UNROLL)

    def dilation4(idx, carry):
        r = idx // 4
        j = idx % 4
        q = q4[pl.ds(pl.multiple_of(quarter * r + blk * j, blk), blk), :]
        start = pl.multiple_of(2 * quarter * r + quarter + blk * (j - 1), blk)
        k = k4[pl.ds(start, 2 * blk), :]
        v = v4[pl.ds(start, 2 * blk), :]
        nat = pl.ds(r + 4 * blk * j, blk, stride=4)
        old = [(acc_n[h, nat, :], m_n[h, nat, :]) for h in range(2)]
        outs = _attn_block(q, k, v, valid_mask(has_prev_window | (j > 0)), old)
        rows = pl.ds(pl.multiple_of(quarter * r + blk * j, blk), blk)
        for h in range(2):
            acc_d[h, rows, :] = outs[h][0]
            m_d[h, rows, :] = outs[h][1]
        return carry

    lax.fori_loop(0, n_blocks, dilation4, 0, unroll=ATTN_UNROLL)

    def dilation16(res, carry):
        r4 = res % 4
        c = res // 4
        rows = pl.ds(quarter * r4 + c, blk, stride=4)
        q = q4[rows, :]
        keys = pl.ds(2 * quarter * r4 + c, 2 * blk, stride=4)
        k = k4[keys, :]
        v = v4[keys, :]
        old = [(acc_d[h, rows, :], m_d[h, rows, :]) for h in range(2)]
        outs = _attn_block(q, k, v, valid_mask(has_prev_window), old)
        for h in range(2):
            acc_d[h, rows, :] = outs[h][0]
            m_d[h, rows, :] = outs[h][1]
        return carry

    lax.fori_loop(0, n_blocks, dilation16, 0, unroll=ATTN_UNROLL)

    lane = lax.broadcasted_iota(jnp.int32, (blk, LANES), 1)
    head0 = lane < HEAD_DIM

    def finalize(idx, carry):
        r = idx // 4
        j = idx % 4
        rows = pl.ds(pl.multiple_of(quarter * r + blk * j, blk), blk)
        a0 = acc_d[0, rows, :]
        a1 = acc_d[1, rows, :]
        num = jnp.where(head0, a0, a1)
        den = pltpu.roll(jnp.where(head0, a1, a0), HEAD_DIM, 1)
        acc_n[0, pl.ds(r + 4 * blk * j, blk, stride=4), :] = num / den
        return carry

    lax.fori_loop(0, n_blocks, finalize, 0, unroll=ATTN_UNROLL)
    o_ref[0] = acc_n[0].astype(o_ref.dtype)


def _attention(qk, proj_qkv, batch, seq):
    win = ATTN_WINDOW
    n_pairs = ATTN_WIDTH // LANES
    qk3 = qk.reshape(batch, seq, 2 * ATTN_WIDTH)
    pv3 = proj_qkv.reshape(batch, seq, 3 * ATTN_WIDTH)
    blockspec = lambda fn: pl.BlockSpec((1, win, LANES), fn)
    prev = lambda w: jnp.maximum(w - 1, 0)
    scratch = [pltpu.VMEM((2 * win, LANES), F32)] * 4 + [pltpu.VMEM((win, LANES), F32)] + \
              [pltpu.VMEM((2, win, LANES), F32)] * 4
    out = pl.pallas_call(
        _attn_kernel,
        grid=(batch, n_pairs, seq // win),
        in_specs=[
            blockspec(lambda b, hp, w: (b, w, hp)),
            blockspec(lambda b, hp, w: (b, prev(w), n_pairs + hp)),
            blockspec(lambda b, hp, w: (b, w, n_pairs + hp)),
            blockspec(lambda b, hp, w: (b, prev(w), 2 * n_pairs + hp)),
            blockspec(lambda b, hp, w: (b, w, 2 * n_pairs + hp)),
        ],
        out_specs=blockspec(lambda b, hp, w: (b, w, hp)),
        out_shape=jax.ShapeDtypeStruct((batch, seq, ATTN_WIDTH), BF16),
        scratch_shapes=scratch,
        compiler_params=_params("parallel", "parallel", "arbitrary"),
        name="dilated_attention",
    )(qk3, qk3, qk3, pv3, pv3)
    return out.reshape(batch * seq, ATTN_WIDTH)


def _head_allreduce(x):
    return x + pltpu.roll(x, 32, 1) + pltpu.roll(x, 64, 1) + pltpu.roll(x, 96, 1)


def _softplus(x):
    return jnp.maximum(x, 0.0) + jnp.log1p(jnp.exp(-jnp.abs(x)))


def _rwkv_prep_kernel(p_ref, halo_ref, mu_ref, w0_ref, a0_ref, kk_ref, ka_ref, wd_ref, wa_ref, wg_ref,
                      r_out, w_out, k_out, v_out, na_out, kb_out, g_out, shift, kmix, *, tm, tiles_per_seq):
    seq_start = (pl.program_id(0) % tiles_per_seq) == 0
    n_chunks = RWKV_WIDTH // LANES

    def mixed(c0):
        cols = slice(c0, c0 + LANES)
        x = p_ref[:, cols]
        shift[SUBLANES - 1:SUBLANES, :] = jnp.where(seq_start, 0.0, halo_ref[SUBLANES - 1:SUBLANES, cols])
        shift[SUBLANES:SUBLANES + tm, :] = x
        prev = shift[SUBLANES - 1:SUBLANES - 1 + tm, :]
        return x + (prev - x) * mu_ref[:, cols]

    lora0 = 3 * RWKV_WIDTH
    th = jnp.tanh(mixed(lora0)).astype(BF16)
    ad = mixed(lora0 + DECAY_LORA).astype(BF16)
    gate0 = lora0 + DECAY_LORA + ICLR_LORA
    sg = jnp.concatenate([jax.nn.sigmoid(mixed(gate0 + c * LANES)) for c in range(GATE_LORA_PAD // LANES)],
                         axis=1).astype(BF16)

    ssq = jnp.zeros((tm, LANES), F32)
    for c in range(n_chunks):
        cols = slice(c * LANES, (c + 1) * LANES)
        km = mixed(RWKV_WIDTH + c * LANES)
        kmix[:, cols] = km
        kk = km * kk_ref[:, cols]
        ssq = ssq + kk * kk
    inv_norm = 1.0 / jnp.maximum(jnp.sqrt(_head_allreduce(ssq)), 1e-12)

    for c in range(n_chunks):
        cols = slice(c * LANES, (c + 1) * LANES)
        rows = slice(c * RWKV_PITCH, c * RWKV_PITCH + tm)
        pad = slice(c * RWKV_PITCH + tm, (c + 1) * RWKV_PITCH)
        for out in (r_out, w_out, k_out, v_out, na_out, kb_out, g_out):
            out[0, pad, :] = jnp.zeros((RWKV_PITCH - tm, LANES), F32)
        r = mixed(c * LANES)
        v = mixed(2 * RWKV_WIDTH + c * LANES)
        k = kmix[:, cols]
        z = w0_ref[:, cols] + jnp.dot(th, wd_ref[:, cols], preferred_element_type=F32)
        w_raw = -_softplus(-z) - 0.5
        a = jax.nn.sigmoid(a0_ref[:, cols] + jnp.dot(ad, wa_ref[:, cols], preferred_element_type=F32))
        kkn = k * kk_ref[:, cols] * inv_norm
        r_out[0, rows, :] = r
        w_out[0, rows, :] = jnp.exp(-jnp.exp(w_raw))
        k_out[0, rows, :] = k * (1.0 + (a - 1.0) * ka_ref[:, cols])
        v_out[0, rows, :] = v
        na_out[0, rows, :] = -kkn
        kb_out[0, rows, :] = kkn * a
        g_out[0, rows, :] = jnp.dot(sg, wg_ref[:, cols], preferred_element_type=F32)


def _rwkv_prep(proj, seq, mu, w0, a0, k_k, k_a, w_decay, w_iclr, w_gate):
    t, width = proj.shape
    tm = RWKV_TILE
    n_chunks = RWKV_WIDTH // LANES
    row = lambda v: v.reshape(1, -1)
    full = lambda a: pl.BlockSpec(a.shape, lambda i: (0, 0))
    halo_blocks = tm // SUBLANES
    params = [row(mu), row(w0), row(a0), row(k_k), row(k_a), w_decay, w_iclr, w_gate]
    out_spec = pl.BlockSpec((1, n_chunks * RWKV_PITCH, LANES), lambda i: (i, 0, 0))
    return pl.pallas_call(
        functools.partial(_rwkv_prep_kernel, tm=tm, tiles_per_seq=seq // tm),
        grid=(t // tm,),
        in_specs=[pl.BlockSpec((tm, width), lambda i: (i, 0)),
                  pl.BlockSpec((SUBLANES, width), lambda i: (jnp.maximum(i * halo_blocks - 1, 0), 0))]
                 + [full(a) for a in params],
        out_specs=[out_spec] * 7,
        out_shape=[jax.ShapeDtypeStruct((t // tm, n_chunks * RWKV_PITCH, LANES), F32)] * 7,
        scratch_shapes=[pltpu.VMEM((tm + SUBLANES, LANES), F32), pltpu.VMEM((tm, RWKV_WIDTH), F32)],
        compiler_params=_params("parallel"),
        name="rwkv_prep",
    )(proj, proj, *params)


def _rwkv_scan_kernel(r_ref, w_ref, k_ref, a_ref, b_ref, v_ref, y_ref, s_ref, *, tm):
    n_chunks = RWKV_WIDTH // LANES
    n_groups = LANES // N_RWKV_HEADS
    lane_group = lax.broadcasted_iota(jnp.int32, (SUBLANES, LANES), 1) // N_RWKV_HEADS

    @pl.when(pl.program_id(0) == 0)
    def _():
        s_ref[...] = jnp.zeros_like(s_ref)

    def row(ref, bi, t, c):
        return ref[bi, 0, pl.ds(c * RWKV_PITCH + t, 1), :]

    def pick_groups(tiles):
        out = tiles[n_groups - 1]
        for g in range(n_groups - 2, -1, -1):
            out = jnp.where(lane_group == g, tiles[g], out)
        return out

    def step(t, carry):
        for bi in range(s_ref.shape[0]):
            v_rows = {}
            for half in range(2):
                x = v_ref[bi, 0, pl.ds(SUBLANES * half * RWKV_PITCH + t, SUBLANES, stride=RWKV_PITCH), :]
                rolled = [x] + [pltpu.roll(x, N_RWKV_HEADS * j, 1) for j in range(1, n_groups)]
                for q in range(n_groups):
                    v_rows[(q, half)] = pick_groups([rolled[(g - q) % n_groups] for g in range(n_groups)])
            vt = jnp.concatenate([v_rows[(q, half)] for q in range(n_groups) for half in range(2)], axis=0)
            part = None
            for c in range(n_chunks):
                cols = slice(c * LANES, (c + 1) * LANES)
                term = s_ref[bi, :, cols] * row(a_ref, bi, t, c)
                part = term if part is None else part + term
            sa = _head_allreduce(part)
            ypart = None
            for c in range(n_chunks):
                cols = slice(c * LANES, (c + 1) * LANES)
                s_new = (s_ref[bi, :, cols] * row(w_ref, bi, t, c) + sa * row(b_ref, bi, t, c)
                         + vt * row(k_ref, bi, t, c))
                s_ref[bi, :, cols] = s_new
                term = s_new * row(r_ref, bi, t, c)
                ypart = term if ypart is None else ypart + term
            y = _head_allreduce(ypart)
            for half in range(2):
                tiles = [y[2 * SUBLANES * g + SUBLANES * half:2 * SUBLANES * g + SUBLANES * (half + 1), :]
                         for g in range(n_groups)]
                rows = pl.ds(SUBLANES * half * RWKV_PITCH + t, SUBLANES, stride=RWKV_PITCH)
                y_ref[bi, 0, rows, :] = pick_groups(tiles)
        return carry

    for bi in range(s_ref.shape[0]):
        for c in range(n_chunks):
            y_ref[bi, 0, c * RWKV_PITCH + tm:(c + 1) * RWKV_PITCH, :] = jnp.zeros((RWKV_PITCH - tm, LANES), F32)
    lax.fori_loop(0, tm, step, 0, unroll=4)


def _rwkv_scan(r, w, k, na, kb, v, batch, seq):
    tm = RWKV_TILE
    rows = (RWKV_WIDTH // LANES) * RWKV_PITCH
    shape4 = (batch, seq // tm, rows, LANES)
    spec = pl.BlockSpec((batch, 1, rows, LANES), lambda i: (0, i, 0, 0))
    y = pl.pallas_call(
        functools.partial(_rwkv_scan_kernel, tm=tm),
        grid=(seq // tm,),
        in_specs=[spec] * 6,
        out_specs=spec,
        out_shape=jax.ShapeDtypeStruct(shape4, F32),
        scratch_shapes=[pltpu.VMEM((batch, HEAD_DIM, RWKV_WIDTH), F32)],
        compiler_params=_params("arbitrary"),
        name="rwkv_scan",
    )(*[x.reshape(shape4) for x in (r, w, k, na, kb, v)])
    return y.reshape(batch * seq // tm, rows, LANES)


def _rwkv_post_kernel(y_ref, r_ref, k_ref, v_ref, g_ref, lnw_ref, lnb_ref, rk_ref, o_ref, *, tm):
    n_chunks = RWKV_WIDTH // LANES
    total = None
    bonus = None
    for c in range(n_chunks):
        rows = slice(c * RWKV_PITCH, c * RWKV_PITCH + tm)
        y = y_ref[0, rows, :]
        total = y if total is None else total + y
        term = r_ref[0, rows, :] * k_ref[0, rows, :] * rk_ref[:, c * LANES:(c + 1) * LANES]
        bonus = term if bonus is None else bonus + term
    mean = _head_allreduce(total) * (1.0 / HEAD_DIM)
    bonus = _head_allreduce(bonus)
    sq = None
    for c in range(n_chunks):
        d = y_ref[0, c * RWKV_PITCH:c * RWKV_PITCH + tm, :] - mean
        sq = d * d if sq is None else sq + d * d
    rstd = lax.rsqrt(_head_allreduce(sq) * (1.0 / HEAD_DIM) + RWKV_GN_EPS)
    for c in range(n_chunks):
        rows = slice(c * RWKV_PITCH, c * RWKV_PITCH + tm)
        cols = slice(c * LANES, (c + 1) * LANES)
        yn = (y_ref[0, rows, :] - mean) * rstd * lnw_ref[:, cols] + lnb_ref[:, cols]
        o_ref[:, cols] = ((yn + bonus * v_ref[0, rows, :]) * g_ref[0, rows, :]).astype(o_ref.dtype)


def _rwkv_post(y, r, k, v, g, ln_w, ln_b, r_k):
    tm = RWKV_TILE
    n_tiles, rows, _ = y.shape
    tile = pl.BlockSpec((1, rows, LANES), lambda i: (i, 0, 0))
    vec = pl.BlockSpec((1, RWKV_WIDTH), lambda i: (0, 0))
    return pl.pallas_call(
        functools.partial(_rwkv_post_kernel, tm=tm),
        grid=(n_tiles,),
        in_specs=[tile] * 5 + [vec] * 3,
        out_specs=pl.BlockSpec((tm, RWKV_WIDTH), lambda i: (i, 0)),
        out_shape=jax.ShapeDtypeStruct((n_tiles * tm, RWKV_WIDTH), BF16),
        compiler_params=_params("parallel"),
        name="rwkv_post",
    )(y, r, k, v, g, ln_w.reshape(1, -1), ln_b.reshape(1, -1), r_k.reshape(1, -1))


def _mlp_up_kernel(x_ref, wg_ref, wu_ref, cwg_ref, cwu_ref, cbg_ref, cbu_ref, o_ref,
                   work_g, work_u, carry_g, carry_u, *, tm, tiles_per_seq):
    seq_start = (pl.program_id(0) % tiles_per_seq) == 0
    j = pl.program_id(1)
    lo = SUBLANES - (CONV_WIDTH - 1)
    part = tm // MLP_ROW_SPLIT
    work_g[0:SUBLANES, :] = jnp.where(seq_start, 0.0, carry_g[j])
    work_u[0:SUBLANES, :] = jnp.where(seq_start, 0.0, carry_u[j])

    def conv(x, r0, w_ref, work, cw_ref, cb_ref):
        u = _mxu_dot(x, w_ref[...])
        work[SUBLANES + r0:SUBLANES + r0 + part, :] = u
        out = work[lo + r0:lo + r0 + part, :] * cw_ref[0:1, :]
        out = out + work[lo + 1 + r0:lo + 1 + r0 + part, :] * cw_ref[1:2, :]
        out = out + u * cw_ref[2:3, :]
        return out + cb_ref[...]

    for p in range(MLP_ROW_SPLIT):
        r0 = p * part
        x = x_ref[r0:r0 + part, :]
        gate = conv(x, r0, wg_ref, work_g, cwg_ref, cbg_ref)
        up = conv(x, r0, wu_ref, work_u, cwu_ref, cbu_ref)
        o_ref[r0:r0 + part, :] = (gate * jax.nn.sigmoid(gate) * up).astype(o_ref.dtype)
    carry_g[j] = work_g[tm:tm + SUBLANES, :]
    carry_u[j] = work_u[tm:tm + SUBLANES, :]


def _mlp_up(x, w, conv_w, conv_b, seq, *, tm=1024, tn=256):
    m, k = x.shape
    n = w.shape[1] // 2
    n_tiles = n // tn
    gate = lambda shape: pl.BlockSpec(shape, lambda i, j: (0, j))
    up = lambda shape: pl.BlockSpec(shape, lambda i, j: (0, j + n_tiles))
    conv_b = conv_b.reshape(1, 2 * n)
    return pl.pallas_call(
        functools.partial(_mlp_up_kernel, tm=tm, tiles_per_seq=seq // tm),
        grid=(m // tm, n_tiles),
        in_specs=[pl.BlockSpec((tm, k), lambda i, j: (i, 0)), gate((k, tn)), up((k, tn)),
                  gate((CONV_WIDTH, tn)), up((CONV_WIDTH, tn)), gate((1, tn)), up((1, tn))],
        out_specs=pl.BlockSpec((tm, tn), lambda i, j: (i, j)),
        out_shape=jax.ShapeDtypeStruct((m, n), BF16),
        scratch_shapes=[pltpu.VMEM((tm + SUBLANES, tn), F32),
                        pltpu.VMEM((tm + SUBLANES, tn), F32),
                        pltpu.VMEM((n_tiles, SUBLANES, tn), F32),
                        pltpu.VMEM((n_tiles, SUBLANES, tn), F32)],
        compiler_params=_params("arbitrary", "arbitrary"),
        name="mlp_up_conv_gate",
    )(x, w, w, conv_w, conv_w, conv_b, conv_b)


def _ple_embed_kernel(p_ref, w_ref, g_ref, o_ref):
    e = _mxu_dot(p_ref[...].astype(BF16), w_ref[...])
    ms = jnp.mean(e * e, axis=-1, keepdims=True)
    o_ref[...] = e * lax.rsqrt(ms + NORM_EPS) * g_ref[...]


def _ple_embed(p, w, g, *, tm=256):
    t, k = p.shape
    d = w.shape[1]
    return pl.pallas_call(
        _ple_embed_kernel,
        grid=(t // tm,),
        in_specs=[pl.BlockSpec((tm, k), lambda i: (i, 0)), pl.BlockSpec((k, d), lambda i: (0, 0)),
                  pl.BlockSpec((1, d), lambda i: (0, 0))],
        out_specs=pl.BlockSpec((tm, d), lambda i: (i, 0)),
        out_shape=jax.ShapeDtypeStruct((t, d), F32),
        compiler_params=_params("parallel"),
        name="ple_embed",
    )(p, w, g.reshape(1, d))


def _to_k_major(w):
    lead = w.shape[:-1]
    return w.reshape(*lead, N_RWKV_HEADS, HEAD_DIM).swapaxes(-1, -2).reshape(*lead, RWKV_WIDTH)


def _split_rwkv_cols(a, pad_value=0.0):
    pad = jnp.full(a.shape[:-1] + (GATE_LORA_PAD - GATE_LORA,), pad_value, a.dtype)
    blocks = [_to_k_major(a[..., i * RWKV_WIDTH:(i + 1) * RWKV_WIDTH]) for i in range(3)]
    return jnp.concatenate(blocks + [a[..., 3 * RWKV_WIDTH:], pad], axis=-1)


def kernel(x, p, positions, attn_norm_g, w_in, q_norm_g, k_norm_g, rwkv_mu, w0, w_decay_up, a0, w_iclr_up,
           w_gate_up, k_k, k_a, r_k, ln_x_w, ln_x_b, w_out, mlp_norm_g, w_mlp_up, conv_w, conv_b, w_mlp_down,
           w_ple_proj, ple_norm_g, w_ple_gate):
    batch, seq, d_model = x.shape
    assert w_in.shape[0] == 1 and seq % ATTN_WINDOW == 0
    t = batch * seq
    x2 = x.reshape(t, d_model)

    w_rwkv = _split_rwkv_cols(w_in[0][:, 3 * ATTN_WIDTH:]).astype(BF16)
    mu = _split_rwkv_cols(rwkv_mu[0])
    w_gate_lora = jnp.concatenate([_to_k_major(w_gate_up[0]),
                                   jnp.zeros((GATE_LORA_PAD - GATE_LORA, RWKV_WIDTH), F32)]).astype(BF16)
    w_out_rwkv = (w_out[0][ATTN_WIDTH:].reshape(N_RWKV_HEADS, HEAD_DIM, d_model).swapaxes(0, 1)
                  .reshape(RWKV_WIDTH, d_model).astype(BF16))

    xn = _rmsnorm_bf16(x2, attn_norm_g[0])
    proj_qkv = _matmul(xn, w_in[0], n=3 * ATTN_WIDTH, tm=1024, tn=512, name="in_proj_attn")
    proj_rwkv = _matmul(xn, w_rwkv, tm=1024, tn=768, name="in_proj_rwkv")

    qk = _qk_prep(proj_qkv, positions, q_norm_g[0], k_norm_g[0])
    attn = _attention(qk, proj_qkv, batch, seq)

    r, w, k, v, na, kb, g = _rwkv_prep(
        proj_rwkv, seq, mu, _to_k_major(w0[0]), _to_k_major(a0[0]), _to_k_major(k_k[0]), _to_k_major(k_a[0]),
        _to_k_major(w_decay_up[0]).astype(BF16), _to_k_major(w_iclr_up[0]).astype(BF16), w_gate_lora)
    y = _rwkv_scan(r, w, k, na, kb, v, batch, seq)
    rwkv = _rwkv_post(y, r, k, v, g, _to_k_major(ln_x_w[0]), _to_k_major(ln_x_b[0]),
                      r_k[0].T.reshape(RWKV_WIDTH))

    h1 = _out_proj(attn, rwkv, w_out[0], w_out_rwkv, x2)

    hn = _rmsnorm_bf16(h1, mlp_norm_g[0])
    act = _mlp_up(hn, w_mlp_up[0], conv_w[0], conv_b[0], seq)
    h2, h2_bf16 = _matmul(act, w_mlp_down[0].astype(BF16), tm=512, tn=512, residual=h1, emit_bf16=True,
                          name="mlp_down")

    e = _ple_embed(p[0].reshape(t, -1), w_ple_proj[0], ple_norm_g[0])
    out = _matmul(h2_bf16, w_ple_gate[0], tm=1024, tn=512, residual=h2, gate_e=e, name="ple_gate")
    return out.reshape(batch, seq, d_model)
```
